```python
import jax, jax.numpy as jnp
from jax import lax
import numpy as np

D_MODEL = 1024
BATCH = 32
SEQ = 256
DEPTH = 4
DEC_BATCH = 8
DEC_SEQ = 4096
PAST_LEN = 512

GRID_W = 64
N_MIXERS = 3
N_ATTN_LAYERS = (DEPTH + 2) // 3
N_POOL_LAYERS = (DEPTH + 1) // 3
N_FOURIER_LAYERS = DEPTH // 3

MLA_HEADS = 16
QK_NOPE_DIM = 64
QK_ROPE_DIM = 32
V_HEAD_DIM = 64
Q_LORA_RANK = 384
KV_LORA_RANK = 256
ROPE_THETA = 10000.0
Q_BLOCK = 128

POOL_WINDOWS = (2, 4, 8, 16)
N_POOL_GROUPS = 4
POOL_GROUP_DIM = D_MODEL // N_POOL_GROUPS

N_FOURIER_GROUPS = 4
FOURIER_GROUP_DIM = D_MODEL // N_FOURIER_GROUPS

PEER_HEADS = 8
PEER_N_KEYS = 128
PEER_N_EXPERTS = PEER_N_KEYS * PEER_N_KEYS
PEER_QUERY_DIM = 256
PEER_HALF = PEER_QUERY_DIM // 2
PEER_TOPK = 16
PEER_CHUNK = 128

DEEPNORM_ALPHA = (2 * DEPTH) ** 0.25
DEEPNORM_BETA = (8 * DEPTH) ** -0.25
LN_EPS = 1e-5
RMS_EPS = 1e-6

kernel_name = 'hybrid_mla_pool_fnet_peer_diffusion_step'


def layer_norm(x, g, b):
    xf = x.astype(jnp.float32)
    mu = jnp.mean(xf, axis=-1, keepdims=True)
    var = jnp.mean(jnp.square(xf - mu), axis=-1, keepdims=True)
    y = (xf - mu) * lax.rsqrt(var + LN_EPS)
    return (y * g.astype(jnp.float32) + b.astype(jnp.float32)).astype(x.dtype)


def rms_norm(x, g):
    xf = x.astype(jnp.float32)
    y = xf * lax.rsqrt(jnp.mean(xf * xf, axis=-1, keepdims=True) + RMS_EPS)
    return (y * g.astype(jnp.float32)).astype(x.dtype)


def adaln_params(cond, w, b):
    m = cond @ w + b
    return jnp.split(m[:, None, :], 6, axis=-1)


def axial_rope_tables(length):
    rows = length // GRID_W
    t = jnp.arange(rows * GRID_W)
    row = (t // GRID_W).astype(jnp.float32)
    col = (t % GRID_W).astype(jnp.float32)
    half = QK_ROPE_DIM // 2
    inv_freq = 1.0 / jnp.power(ROPE_THETA, jnp.arange(0, half, 2, dtype=jnp.float32) / half)
    ang = jnp.concatenate([row[:, None] * inv_freq, col[:, None] * inv_freq], axis=-1)
    return jnp.cos(ang), jnp.sin(ang)


def apply_axial_rope(x, cos, sin):
    n = QK_ROPE_DIM // 4
    xs = x.astype(jnp.float32).reshape(x.shape[:-1] + (2, 2, n))
    c = cos.reshape(cos.shape[0], 1, 2, n)
    s = sin.reshape(sin.shape[0], 1, 2, n)
    x1, x2 = xs[..., 0, :], xs[..., 1, :]
    out = jnp.stack([x1 * c - x2 * s, x1 * s + x2 * c], axis=-2)
    return out.reshape(x.shape).astype(x.dtype)


def blocked_attention(q, k, v):
    B, Lq, H, dq = q.shape
    nb = Lq // Q_BLOCK
    qb = jnp.swapaxes(q.reshape(B, nb, Q_BLOCK, H, dq), 0, 1)
    scale = dq ** -0.5

    def one_block(qi):
        s = jnp.einsum('bqhd,bkhd->bhqk', qi, k, preferred_element_type=jnp.float32) * scale
        p = jax.nn.softmax(s, axis=-1).astype(v.dtype)
        return jnp.einsum('bhqk,bkhd->bqhd', p, v)

    out = lax.map(one_block, qb)
    return jnp.swapaxes(out, 0, 1).reshape(B, Lq, H, v.shape[-1])


def mla_queries(h, wdq, gq, wuq):
    B, L, _ = h.shape
    cq = rms_norm(h @ wdq, gq)
    q = (cq @ wuq).reshape(B, L, MLA_HEADS, QK_NOPE_DIM + QK_ROPE_DIM)
    return q[..., :QK_NOPE_DIM], q[..., QK_NOPE_DIM:]


def mla_compress_kv(h, wdkv, gkv):
    kv = h @ wdkv
    return rms_norm(kv[..., :KV_LORA_RANK], gkv), kv[..., KV_LORA_RANK:]


def mla_expand_kv(c_kv, k_pe, wukv):
    B, L, _ = c_kv.shape
    kv = (c_kv @ wukv).reshape(B, L, MLA_HEADS, QK_NOPE_DIM + V_HEAD_DIM)
    k_nope, v = kv[..., :QK_NOPE_DIM], kv[..., QK_NOPE_DIM:]
    k_pe_h = jnp.broadcast_to(k_pe[:, :, None, :], (B, L, MLA_HEADS, QK_ROPE_DIM))
    return jnp.concatenate([k_nope, k_pe_h], axis=-1), v


def mla_context(h, wdq, gq, wuq, wdkv, gkv, wukv, wo):
    B, L, _ = h.shape
    q_nope, q_pe = mla_queries(h, wdq, gq, wuq)
    c_kv, k_pe = mla_compress_kv(h, wdkv, gkv)
    k, v = mla_expand_kv(c_kv, k_pe, wukv)
    o = blocked_attention(jnp.concatenate([q_nope, q_pe], axis=-1), k, v)
    return o.reshape(B, L, MLA_HEADS * V_HEAD_DIM) @ wo, c_kv, k_pe


def mla_latent(h, ckv_ctx, kpe_ctx, wdq, gq, wuq, wdkv, gkv, wukv, wo):
    B, L, _ = h.shape
    cos, sin = axial_rope_tables(L)
    q_nope, q_pe = mla_queries(h, wdq, gq, wuq)
    q_pe = apply_axial_rope(q_pe, cos, sin)
    c_kv, k_pe = mla_compress_kv(h, wdkv, gkv)
    k_pe = apply_axial_rope(k_pe[:, :, None, :], cos, sin)[:, :, 0, :]
    k_lat, v_lat = mla_expand_kv(c_kv, k_pe, wukv)
    k_ctx, v_ctx = mla_expand_kv(ckv_ctx, kpe_ctx, wukv)
    k = jnp.concatenate([k_ctx, k_lat], axis=1)
    v = jnp.concatenate([v_ctx, v_lat], axis=1)
    o = blocked_attention(jnp.concatenate([q_nope, q_pe], axis=-1), k, v)
    return o.reshape(B, L, MLA_HEADS * V_HEAD_DIM) @ wo


def multiscale_pool(h, w_maps, scale):
    B, L, D = h.shape
    t = jnp.arange(L)
    hg = h.astype(jnp.float32).reshape(B, L, N_POOL_GROUPS, POOL_GROUP_DIM)
    cs = jnp.concatenate([jnp.zeros((B, 1, N_POOL_GROUPS, POOL_GROUP_DIM), jnp.float32),
                          jnp.cumsum(hg, axis=1)], axis=1)
    outs = []
    for g, w in enumerate(POOL_WINDOWS):
        lo = jnp.maximum(t - w // 2, 0)
        hi = jnp.minimum(t + (w - w // 2) - 1, L - 1)
        cnt = (hi - lo + 1).astype(jnp.float32)[:, None]
        mean = (cs[:, hi + 1, g] - cs[:, lo, g]) / cnt
        outs.append((mean - hg[:, :, g]).astype(h.dtype) @ w_maps[g])
    return jnp.concatenate(outs, axis=-1) * scale


def fourier_mix(h, w, b):
    B, L, D = h.shape
    hg = h.astype(jnp.float32).reshape(B, L, N_FOURIER_GROUPS, FOURIER_GROUP_DIM)
    f = jnp.fft.fftn(hg, axes=(1, 3), norm='ortho').real
    return f.reshape(B, L, D).astype(h.dtype) @ w + b


def peer_ffn(h, wq, subkeys, u, v):
    B, L, D = h.shape
    chunks = h.reshape(-1, PEER_CHUNK, D)
    kk = PEER_TOPK * PEER_TOPK

    def one_chunk(xc):
        q = (xc @ wq).reshape(PEER_CHUNK, PEER_HEADS, 2, PEER_HALF)
        s = jnp.einsum('thsd,hsnd->thsn', q, subkeys, preferred_element_type=jnp.float32)
        s_top, i_top = lax.top_k(s, PEER_TOPK)
        cand = (s_top[:, :, 0, :, None] + s_top[:, :, 1, None, :]).reshape(PEER_CHUNK, PEER_HEADS, kk)
        cand_idx = (i_top[:, :, 0, :, None] * PEER_N_KEYS + i_top[:, :, 1, None, :]).reshape(PEER_CHUNK, PEER_HEADS, kk)
        best, pos = lax.top_k(cand, PEER_TOPK)
        idx = jnp.take_along_axis(cand_idx, pos, axis=-1)
        g = jax.nn.softmax(best, axis=-1).astype(xc.dtype)
        a = jax.nn.gelu(jnp.einsum('td,thkd->thk', xc, u[idx]), approximate=False)
        return jnp.einsum('thk,thkd->td', g * a, v[idx])

    return lax.map(one_chunk, chunks).reshape(B, L, D)


def setup_inputs(seed: int = 0) -> dict:
    key = jax.random.key(seed)
    ks = jax.random.split(key, 32)
    f32 = jnp.float32
    nrm = lambda k, shape, s: jax.random.normal(k, shape, f32) * s
    D = D_MODEL
    return {
        'x_prompt': nrm(ks[0], (BATCH, SEQ, D), 1.0),
        'x_sample': nrm(ks[1], (DEC_BATCH, DEC_SEQ, D), 1.0),
        'cache_ckv': nrm(ks[2], (DEC_BATCH, N_ATTN_LAYERS, PAST_LEN, KV_LORA_RANK), 1.0),
        'cache_kpe': nrm(ks[3], (DEC_BATCH, N_ATTN_LAYERS, PAST_LEN, QK_ROPE_DIM), 1.0),
        'c': nrm(ks[4], (DEC_BATCH, D), 1.0),
        'c_ctx': nrm(ks[5], (D,), 1.0),
        'ada_w': nrm(ks[6], (DEPTH, D, 6 * D), D ** -0.5),
        'ada_b': nrm(ks[7], (DEPTH, 6 * D), 0.02),
        'ln_mix_g': 1.0 + nrm(ks[8], (DEPTH, D), 0.02),
        'ln_mix_b': nrm(ks[9], (DEPTH, D), 0.02),
        'ln_ffn_g': 1.0 + nrm(ks[10], (DEPTH, D), 0.02),
        'ln_ffn_b': nrm(ks[11], (DEPTH, D), 0.02),
        'mla_wdq': nrm(ks[12], (N_ATTN_LAYERS, D, Q_LORA_RANK), D ** -0.5),
        'mla_q_norm_g': 1.0 + nrm(ks[13], (N_ATTN_LAYERS, Q_LORA_RANK), 0.02),
        'mla_wuq': nrm(ks[14], (N_ATTN_LAYERS, Q_LORA_RANK, MLA_HEADS * (QK_NOPE_DIM + QK_ROPE_DIM)), Q_LORA_RANK ** -0.5),
        'mla_wdkv': nrm(ks[15], (N_ATTN_LAYERS, D, KV_LORA_RANK + QK_ROPE_DIM), D ** -0.5),
        'mla_kv_norm_g': 1.0 + nrm(ks[16], (N_ATTN_LAYERS, KV_LORA_RANK), 0.02),
        'mla_wukv': nrm(ks[17], (N_ATTN_LAYERS, KV_LORA_RANK, MLA_HEADS * (QK_NOPE_DIM + V_HEAD_DIM)), KV_LORA_RANK ** -0.5),
        'mla_wo': nrm(ks[18], (N_ATTN_LAYERS, MLA_HEADS * V_HEAD_DIM, D), DEEPNORM_BETA * (MLA_HEADS * V_HEAD_DIM) ** -0.5),
        'pool_w': nrm(ks[19], (N_POOL_LAYERS, N_POOL_GROUPS, POOL_GROUP_DIM, POOL_GROUP_DIM), DEEPNORM_BETA * POOL_GROUP_DIM ** -0.5),
        'pool_scale': 1.0 + nrm(ks[20], (N_POOL_LAYERS, D), 0.02),
        'fnet_w': nrm(ks[21], (N_FOURIER_LAYERS, D, D), DEEPNORM_BETA * D ** -0.5),
        'fnet_b': nrm(ks[22], (N_FOURIER_LAYERS, D), 0.02),
        'peer_wq': nrm(ks[23], (DEPTH, D, PEER_HEADS * PEER_QUERY_DIM), D ** -0.5),
        'peer_subkeys': nrm(ks[24], (DEPTH, PEER_HEADS, 2, PEER_N_KEYS, PEER_HALF), PEER_HALF ** -0.5),
        'peer_u': nrm(ks[25], (DEPTH, PEER_N_EXPERTS, D), D ** -0.5),
        'peer_v': nrm(ks[26], (DEPTH, PEER_N_EXPERTS, D), DEEPNORM_BETA),
    }


def reference(x_prompt, x_sample, cache_ckv, cache_kpe, c, c_ctx,
              ada_w, ada_b, ln_mix_g, ln_mix_b, ln_ffn_g, ln_ffn_b,
              mla_wdq, mla_q_norm_g, mla_wuq, mla_wdkv, mla_kv_norm_g, mla_wukv, mla_wo,
              pool_w, pool_scale, fnet_w, fnet_b,
              peer_wq, peer_subkeys, peer_u, peer_v):
    cond_ctx = jax.nn.silu(c_ctx)[None, :]
    cond_lat = jax.nn.silu(c)
    xp, xs = x_prompt, x_sample
    new_ckv, new_kpe = [], []
    for i in range(DEPTH):
        kind, j = i % N_MIXERS, i // N_MIXERS
        sh1_p, sc1_p, g1_p, sh2_p, sc2_p, g2_p = adaln_params(cond_ctx, ada_w[i], ada_b[i])
        sh1_s, sc1_s, g1_s, sh2_s, sc2_s, g2_s = adaln_params(cond_lat, ada_w[i], ada_b[i])
        hp = xp * (1.0 + sc1_p) + sh1_p
        hs = xs * (1.0 + sc1_s) + sh1_s
        if kind == 0:
            yp, ckv, kpe = mla_context(hp, mla_wdq[j], mla_q_norm_g[j], mla_wuq[j], mla_wdkv[j],
                                       mla_kv_norm_g[j], mla_wukv[j], mla_wo[j])
            ys = mla_latent(hs, cache_ckv[:, j], cache_kpe[:, j], mla_wdq[j], mla_q_norm_g[j], mla_wuq[j],
                            mla_wdkv[j], mla_kv_norm_g[j], mla_wukv[j], mla_wo[j])
            new_ckv.append(ckv)
            new_kpe.append(kpe)
        elif kind == 1:
            yp = multiscale_pool(hp, pool_w[j], pool_scale[j])
            ys = multiscale_pool(hs, pool_w[j], pool_scale[j])
        else:
            yp = fourier_mix(hp, fnet_w[j], fnet_b[j])
            ys = fourier_mix(hs, fnet_w[j], fnet_b[j])
        xp = layer_norm(DEEPNORM_ALPHA * xp + g1_p * yp, ln_mix_g[i], ln_mix_b[i])
        xs = layer_norm(DEEPNORM_ALPHA * xs + g1_s * ys, ln_mix_g[i], ln_mix_b[i])
        hp = xp * (1.0 + sc2_p) + sh2_p
        hs = xs * (1.0 + sc2_s) + sh2_s
        fp = peer_ffn(hp, peer_wq[i], peer_subkeys[i], peer_u[i], peer_v[i])
        fs = peer_ffn(hs, peer_wq[i], peer_subkeys[i], peer_u[i], peer_v[i])
        xp = layer_norm(DEEPNORM_ALPHA * xp + g2_p * fp, ln_ffn_g[i], ln_ffn_b[i])
        xs = layer_norm(DEEPNORM_ALPHA * xs + g2_s * fs, ln_ffn_g[i], ln_ffn_b[i])
    ckv_out = jnp.stack(new_ckv, axis=1)
    kpe_out = jnp.stack(new_kpe, axis=1)
    return (xp, xs, ckv_out, kpe_out)
```

```python
import functools
import math

import jax
import jax.numpy as jnp
from jax import lax
from jax.experimental import pallas as pl
from jax.experimental.pallas import tpu as pltpu

F32 = jnp.float32
MM_DTYPE = jnp.bfloat16

D_MODEL = 1024
DEPTH = 4
GRID_W = 64
N_MIXERS = 3

MLA_HEADS = 16
QK_NOPE_DIM = 64
QK_ROPE_DIM = 32
V_HEAD_DIM = 64
Q_LORA_RANK = 384
KV_LORA_RANK = 256
ROPE_THETA = 10000.0
HEAD_PAD = 128
ROPE_OFF = QK_NOPE_DIM

POOL_WINDOWS = (2, 4, 8, 16)
N_POOL_GROUPS = 4
POOL_GROUP_DIM = D_MODEL // N_POOL_GROUPS
POOL_HALO = 8

N_FOURIER_GROUPS = 4
FOURIER_GROUP_DIM = D_MODEL // N_FOURIER_GROUPS

PEER_HEADS = 8
PEER_N_KEYS = 128
PEER_N_EXPERTS = PEER_N_KEYS * PEER_N_KEYS
PEER_QUERY_DIM = 256
PEER_HALF = PEER_QUERY_DIM // 2
PEER_TOPK = 16

DEEPNORM_ALPHA = (2 * DEPTH) ** 0.25
LN_EPS = 1e-5
RMS_EPS = 1e-6

LANES = 128
N_COND_ROWS = 16
VMEM_LIMIT = 56 * 1024 * 1024


def _nn(a, b):
    return jnp.dot(a, b, preferred_element_type=F32)


def _nt(a, b):
    return lax.dot_general(a, b, (((1,), (1,)), ((), ())), preferred_element_type=F32)


def _params(*sem):
    return pltpu.CompilerParams(dimension_semantics=sem, vmem_limit_bytes=VMEM_LIMIT)


def _layer_norm(z, g, b):
    mu = jnp.mean(z, axis=-1, keepdims=True)
    d = z - mu
    var = jnp.mean(d * d, axis=-1, keepdims=True)
    return d * lax.rsqrt(var + LN_EPS) * g + b


def _rms_norm(x, g):
    return x * lax.rsqrt(jnp.mean(x * x, axis=-1, keepdims=True) + RMS_EPS) * g


def _full(shape):
    n = len(shape)
    return pl.BlockSpec(shape, lambda *_: (0,) * n)


def _mod_spec(cond_fn):
    return pl.BlockSpec((1, 6, D_MODEL), lambda i, *_: (cond_fn(i), 0, 0))


def _cond_fn(per_seq, tile, seq_len):
    if not per_seq:
        return lambda i: 0
    return lambda i: 1 + (i * tile) // seq_len


def _adaln_kernel(cond_ref, w_ref, b_ref, o_ref):
    c = cond_ref[...]
    c = c / (1.0 + jnp.exp(-c))
    o_ref[0] = jnp.dot(c, w_ref[0], preferred_element_type=F32,
                       precision=lax.Precision.HIGHEST) + b_ref[0]


def _adaln(cond, ada_w, ada_b):
    nb = 1536
    out = pl.pallas_call(
        _adaln_kernel,
        out_shape=jax.ShapeDtypeStruct((DEPTH, N_COND_ROWS, 6 * D_MODEL), F32),
        grid=(DEPTH, 6 * D_MODEL // nb),
        in_specs=[
            pl.BlockSpec((N_COND_ROWS, D_MODEL), lambda i, j: (0, 0)),
            pl.BlockSpec((1, D_MODEL, nb), lambda i, j: (i, 0, j)),
            pl.BlockSpec((1, 1, nb), lambda i, j: (i, 0, j)),
        ],
        out_specs=pl.BlockSpec((1, N_COND_ROWS, nb), lambda i, j: (i, 0, j)),
        compiler_params=_params("parallel", "parallel"),
        name="adaln",
    )(cond, ada_w, ada_b.reshape(DEPTH, 1, 6 * D_MODEL))
    return out.reshape(DEPTH, N_COND_ROWS, 6, D_MODEL)


def _mla_proj_kernel(x_ref, mod_ref, cos_ref, sin_ref, wdq_ref, gq_ref, wuq_ref, wuqs_ref,
                     wdkv_ref, gkv_ref, wukvk_ref, wukvv_ref,
                     q_ref, k_ref, v_ref, ckv_ref, kpe_ref):
    h = x_ref[...] * (1.0 + mod_ref[0, 1:2, :]) + mod_ref[0, 0:1, :]
    hb = h.astype(MM_DTYPE)
    cq = _rms_norm(_nn(hb, wdq_ref[...]), gq_ref[...]).astype(MM_DTYPE)
    q = _nn(cq, wuq_ref[...])
    qs = _nn(cq, wuqs_ref[...])
    kv = _nn(hb, wdkv_ref[...])
    ckv = _rms_norm(kv[:, :KV_LORA_RANK], gkv_ref[...])
    kpe = kv[:, KV_LORA_RANK:KV_LORA_RANK + HEAD_PAD]
    kpes = kv[:, KV_LORA_RANK + HEAD_PAD:]
    cos = cos_ref[...]
    sin = sin_ref[...]
    kpe_rot = kpe * cos + kpes * sin
    ckvb = ckv.astype(MM_DTYPE)
    kn = _nn(ckvb, wukvk_ref[...])
    scale = (QK_NOPE_DIM + QK_ROPE_DIM) ** -0.5
    for hd in range(MLA_HEADS):
        sl = slice(hd * HEAD_PAD, (hd + 1) * HEAD_PAD)
        q_ref[:, sl] = ((q[:, sl] * cos + qs[:, sl] * sin) * scale).astype(q_ref.dtype)
        k_ref[:, sl] = (kn[:, sl] + kpe_rot).astype(k_ref.dtype)
    v_ref[...] = _nn(ckvb, wukvv_ref[...]).astype(v_ref.dtype)
    ckv_ref[...] = ckv
    kpe_ref[...] = kpe[:, ROPE_OFF:ROPE_OFF + QK_ROPE_DIM]


def _mla_proj(x, mod, cond_fn, cos, sin, pos_fn, w, tile):
    n = x.shape[0]
    hp = MLA_HEADS * HEAD_PAD
    hv = MLA_HEADS * V_HEAD_DIM
    row = lambda i: (i, 0)
    return pl.pallas_call(
        _mla_proj_kernel,
        out_shape=(
            jax.ShapeDtypeStruct((n, hp), MM_DTYPE),
            jax.ShapeDtypeStruct((n, hp), MM_DTYPE),
            jax.ShapeDtypeStruct((n, hv), MM_DTYPE),
            jax.ShapeDtypeStruct((n, KV_LORA_RANK), F32),
            jax.ShapeDtypeStruct((n, QK_ROPE_DIM), F32),
        ),
        grid=(n // tile,),
        in_specs=[
            pl.BlockSpec((tile, D_MODEL), row),
            _mod_spec(cond_fn),
            pl.BlockSpec((tile, HEAD_PAD), lambda i: (pos_fn(i), 0)),
            pl.BlockSpec((tile, HEAD_PAD), lambda i: (pos_fn(i), 0)),
            _full(w["wdq"].shape), _full(w["gq"].shape), _full(w["wuq"].shape), _full(w["wuqs"].shape),
            _full(w["wdkv"].shape), _full(w["gkv"].shape), _full(w["wukvk"].shape), _full(w["wukvv"].shape),
        ],
        out_specs=(
            pl.BlockSpec((tile, hp), row),
            pl.BlockSpec((tile, hp), row),
            pl.BlockSpec((tile, hv), row),
            pl.BlockSpec((tile, KV_LORA_RANK), row),
            pl.BlockSpec((tile, QK_ROPE_DIM), row),
        ),
        compiler_params=_params("parallel"),
        name="mla_proj",
    )(x, mod, cos, sin, w["wdq"], w["gq"], w["wuq"], w["wuqs"], w["wdkv"], w["gkv"], w["wukvk"], w["wukvv"])


def _mla_ctx_kernel(ckv_ref, kpe_ref, wukvk_ref, wukvv_ref, place_ref, k_ref, v_ref):
    ckvb = ckv_ref[...].astype(MM_DTYPE)
    kn = _nn(ckvb, wukvk_ref[...])
    kpe = _nn(kpe_ref[...].astype(MM_DTYPE), place_ref[...])
    for hd in range(MLA_HEADS):
        sl = slice(hd * HEAD_PAD, (hd + 1) * HEAD_PAD)
        k_ref[:, sl] = (kn[:, sl] + kpe).astype(k_ref.dtype)
    v_ref[...] = _nn(ckvb, wukvv_ref[...]).astype(v_ref.dtype)


def _mla_ctx(cache_ckv, cache_kpe, j, w):
    nb, _, past, _ = cache_ckv.shape
    hp = MLA_HEADS * HEAD_PAD
    hv = MLA_HEADS * V_HEAD_DIM
    return pl.pallas_call(
        _mla_ctx_kernel,
        out_shape=(jax.ShapeDtypeStruct((nb * past, hp), MM_DTYPE),
                   jax.ShapeDtypeStruct((nb * past, hv), MM_DTYPE)),
        grid=(nb,),
        in_specs=[
            pl.BlockSpec((None, None, past, KV_LORA_RANK), lambda b: (b, j, 0, 0)),
            pl.BlockSpec((None, None, past, QK_ROPE_DIM), lambda b: (b, j, 0, 0)),
            _full(w["wukvk"].shape), _full(w["wukvv"].shape), _full(w["place"].shape),
        ],
        out_specs=(pl.BlockSpec((past, hp), lambda b: (b, 0)),
                   pl.BlockSpec((past, hv), lambda b: (b, 0))),
        compiler_params=_params("parallel"),
        name="mla_ctx",
    )(cache_ckv, cache_kpe, w["wukvk"], w["wukvv"], w["place"])


def _attn_kernel(*refs, has_ctx):
    if has_ctx:
        q_ref, k_ref, v_ref, kc_ref, vc_ref, o_ref = refs
    else:
        q_ref, k_ref, v_ref, o_ref = refs
    lane = lax.broadcasted_iota(jnp.int32, (1, 2 * V_HEAD_DIM), 1)
    out = None
    for hh in range(2):
        sl = slice(hh * HEAD_PAD, (hh + 1) * HEAD_PAD)
        own = (lane >= hh * V_HEAD_DIM) & (lane < (hh + 1) * V_HEAD_DIM)
        q = q_ref[:, sl]
        s = _nt(q, k_ref[:, sl])
        m = jnp.max(s, axis=-1, keepdims=True)
        if has_ctx:
            sc = _nt(q, kc_ref[:, sl])
            m = jnp.maximum(m, jnp.max(sc, axis=-1, keepdims=True))
        p = jnp.exp(s - m)
        l = jnp.sum(p, axis=-1, keepdims=True)
        v = v_ref[...]
        o = _nn(p.astype(MM_DTYPE), jnp.where(own, v, jnp.zeros_like(v)))
        if has_ctx:
            pc = jnp.exp(sc - m)
            l = l + jnp.sum(pc, axis=-1, keepdims=True)
            vc = vc_ref[...]
            o = o + _nn(pc.astype(MM_DTYPE), jnp.where(own, vc, jnp.zeros_like(vc)))
        o = o * (1.0 / l)
        out = o if out is None else out + o
    o_ref[...] = out.astype(o_ref.dtype)


def _attention(q, k, v, n_seq, seq_len, tq, ctx=None):
    n = q.shape[0]
    hpairs = MLA_HEADS // 2
    qb = seq_len // tq
    in_specs = [
        pl.BlockSpec((tq, 2 * HEAD_PAD), lambda b, h, i: (b * qb + i, h)),
        pl.BlockSpec((seq_len, 2 * HEAD_PAD), lambda b, h, i: (b, h)),
        pl.BlockSpec((seq_len, 2 * V_HEAD_DIM), lambda b, h, i: (b, h)),
    ]
    args = [q, k, v]
    if ctx is not None:
        kc, vc = ctx
        past = kc.shape[0] // n_seq
        in_specs += [
            pl.BlockSpec((past, 2 * HEAD_PAD), lambda b, h, i: (b, h)),
            pl.BlockSpec((past, 2 * V_HEAD_DIM), lambda b, h, i: (b, h)),
        ]
        args += [kc, vc]
    return pl.pallas_call(
        functools.partial(_attn_kernel, has_ctx=ctx is not None),
        out_shape=jax.ShapeDtypeStruct((n, MLA_HEADS * V_HEAD_DIM), MM_DTYPE),
        grid=(n_seq, hpairs, qb),
        in_specs=in_specs,
        out_specs=pl.BlockSpec((tq, 2 * V_HEAD_DIM), lambda b, h, i: (b * qb + i, h)),
        compiler_params=_params("parallel", "parallel", "arbitrary"),
        name="mla_attn",
    )(*args)


def _post_store(z, mod_ref, lng_ref, lnb_ref, x1_ref, h2_ref):
    x1 = _layer_norm(z, lng_ref[...], lnb_ref[...])
    x1_ref[...] = x1
    h2_ref[...] = (x1 * (1.0 + mod_ref[0, 4:5, :]) + mod_ref[0, 3:4, :]).astype(h2_ref.dtype)


def _mla_post_kernel(o_ref, x_ref, mod_ref, wo_ref, lng_ref, lnb_ref, x1_ref, h2_ref):
    y = _nn(o_ref[...], wo_ref[...])
    z = DEEPNORM_ALPHA * x_ref[...] + mod_ref[0, 2:3, :] * y
    _post_store(z, mod_ref, lng_ref, lnb_ref, x1_ref, h2_ref)


def _mla_post(o, x, mod, cond_fn, wo, lng, lnb, tile):
    n = x.shape[0]
    row = lambda i: (i, 0)
    return pl.pallas_call(
        _mla_post_kernel,
        out_shape=(jax.ShapeDtypeStruct((n, D_MODEL), F32), jax.ShapeDtypeStruct((n, D_MODEL), MM_DTYPE)),
        grid=(n // tile,),
        in_specs=[
            pl.BlockSpec((tile, D_MODEL), row), pl.BlockSpec((tile, D_MODEL), row), _mod_spec(cond_fn),
            _full(wo.shape), _full(lng.shape), _full(lnb.shape),
        ],
        out_specs=(pl.BlockSpec((tile, D_MODEL), row), pl.BlockSpec((tile, D_MODEL), row)),
        compiler_params=_params("parallel"),
        name="mla_post",
    )(o, x, mod, wo, lng, lnb)


def _pool_kernel(x_ref, xp_ref, xn_ref, mod_ref, w_ref, ps_ref, lng_ref, lnb_ref, x1_ref, h2_ref, ext_ref,
                 *, tile, tiles_per_seq):
    i = pl.program_id(0)
    pos = i % tiles_per_seq
    scale1 = 1.0 + mod_ref[0, 1:2, :]
    shift1 = mod_ref[0, 0:1, :]
    x = x_ref[...]
    keep_prev = (pos > 0).astype(F32)
    keep_next = (pos < tiles_per_seq - 1).astype(F32)
    ext_ref[0:POOL_HALO, :] = (xp_ref[...] * scale1 + shift1) * keep_prev
    ext_ref[POOL_HALO:POOL_HALO + tile, :] = x * scale1 + shift1
    ext_ref[POOL_HALO + tile:, :] = (xn_ref[...] * scale1 + shift1) * keep_next
    seq_len = tile * tiles_per_seq
    t = pos * tile + lax.broadcasted_iota(jnp.int32, (tile, POOL_GROUP_DIM), 0)
    ys = []
    for g, w in enumerate(POOL_WINDOWS):
        cs = slice(g * POOL_GROUP_DIM, (g + 1) * POOL_GROUP_DIM)
        tot = None
        for d in range(-(w // 2), w - w // 2):
            piece = ext_ref[POOL_HALO + d:POOL_HALO + d + tile, cs]
            tot = piece if tot is None else tot + piece
        lo = jnp.maximum(t - w // 2, 0)
        hi = jnp.minimum(t + (w - w // 2) - 1, seq_len - 1)
        cnt = (hi - lo + 1).astype(F32)
        diff = tot / cnt - ext_ref[POOL_HALO:POOL_HALO + tile, cs]
        ys.append(_nn(diff.astype(MM_DTYPE), w_ref[g]))
    y = jnp.concatenate(ys, axis=-1) * ps_ref[...]
    z = DEEPNORM_ALPHA * x + mod_ref[0, 2:3, :] * y
    _post_store(z, mod_ref, lng_ref, lnb_ref, x1_ref, h2_ref)


def _pool_layer(x, mod, cond_fn, seq_len, w, ps, lng, lnb, tile):
    n = x.shape[0]
    tps = seq_len // tile
    hb = tile // POOL_HALO
    nblk = n // POOL_HALO
    row = lambda i: (i, 0)
    return pl.pallas_call(
        functools.partial(_pool_kernel, tile=tile, tiles_per_seq=tps),
        out_shape=(jax.ShapeDtypeStruct((n, D_MODEL), F32), jax.ShapeDtypeStruct((n, D_MODEL), MM_DTYPE)),
        grid=(n // tile,),
        in_specs=[
            pl.BlockSpec((tile, D_MODEL), row),
            pl.BlockSpec((POOL_HALO, D_MODEL), lambda i: (jnp.maximum(i * hb - 1, 0), 0)),
            pl.BlockSpec((POOL_HALO, D_MODEL), lambda i: (jnp.minimum((i + 1) * hb, nblk - 1), 0)),
            _mod_spec(cond_fn), _full(w.shape), _full(ps.shape), _full(lng.shape), _full(lnb.shape),
        ],
        out_specs=(pl.BlockSpec((tile, D_MODEL), row), pl.BlockSpec((tile, D_MODEL), row)),
        scratch_shapes=[pltpu.VMEM((tile + 2 * POOL_HALO, D_MODEL), F32)],
        compiler_params=_params("parallel"),
        name="pool_layer",
    )(x, x, x, mod, w, ps, lng, lnb)


def _fnet_chan_kernel(x_ref, mod_ref, cc_ref, sc_ref, hc_ref, hs_ref):
    h = x_ref[...] * (1.0 + mod_ref[0, 1:2, :]) + mod_ref[0, 0:1, :]
    hb = h.astype(MM_DTYPE)
    for g in range(N_FOURIER_GROUPS):
        cs = slice(g * FOURIER_GROUP_DIM, (g + 1) * FOURIER_GROUP_DIM)
        hc_ref[:, cs] = _nn(hb[:, cs], cc_ref[...]).astype(hc_ref.dtype)
        hs_ref[:, cs] = _nn(hb[:, cs], sc_ref[...]).astype(hs_ref.dtype)


def _fnet_chan(x, mod, cond_fn, cc, sc, tile):
    n = x.shape[0]
    row = lambda i: (i, 0)
    return pl.pallas_call(
        _fnet_chan_kernel,
        out_shape=(jax.ShapeDtypeStruct((n, D_MODEL), MM_DTYPE), jax.ShapeDtypeStruct((n, D_MODEL), MM_DTYPE)),
        grid=(n // tile,),
        in_specs=[pl.BlockSpec((tile, D_MODEL), row), _mod_spec(cond_fn), _full(cc.shape), _full(sc.shape)],
        out_specs=(pl.BlockSpec((tile, D_MODEL), row), pl.BlockSpec((tile, D_MODEL), row)),
        compiler_params=_params("parallel"),
        name="fnet_chan",
    )(x, mod, cc, sc)


def _fnet_seq_kernel(cl_ref, sl_ref, hc_ref, hs_ref, x_ref, mod_ref, fw_ref, fb_ref, lng_ref, lnb_ref,
                     x1_ref, h2_ref, acc_ref):
    tt = pl.program_id(2)

    @pl.when(tt == 0)
    def _():
        acc_ref[...] = jnp.zeros_like(acc_ref)

    acc_ref[...] += _nn(cl_ref[...], hc_ref[...]) - _nn(sl_ref[...], hs_ref[...])

    @pl.when(tt == pl.num_programs(2) - 1)
    def _():
        y = _nn(acc_ref[...].astype(MM_DTYPE), fw_ref[...]) + fb_ref[...]
        z = DEEPNORM_ALPHA * x_ref[...] + mod_ref[0, 2:3, :] * y
        _post_store(z, mod_ref, lng_ref, lnb_ref, x1_ref, h2_ref)


def _fnet_seq(hc, hs, x, mod, per_seq, n_seq, seq_len, cl, sl, fw, fb, lng, lnb, tk, tt):
    n = x.shape[0]
    kb = seq_len // tk
    tb = seq_len // tt
    out_row = lambda b, k, t: (b * kb + k, 0)
    cond = (lambda b, k, t: (1 + b, 0, 0)) if per_seq else (lambda b, k, t: (0, 0, 0))
    return pl.pallas_call(
        _fnet_seq_kernel,
        out_shape=(jax.ShapeDtypeStruct((n, D_MODEL), F32), jax.ShapeDtypeStruct((n, D_MODEL), MM_DTYPE)),
        grid=(n_seq, kb, tb),
        in_specs=[
            pl.BlockSpec((tk, tt), lambda b, k, t: (k, t)),
            pl.BlockSpec((tk, tt), lambda b, k, t: (k, t)),
            pl.BlockSpec((tt, D_MODEL), lambda b, k, t: (b * tb + t, 0)),
            pl.BlockSpec((tt, D_MODEL), lambda b, k, t: (b * tb + t, 0)),
            pl.BlockSpec((tk, D_MODEL), out_row),
            pl.BlockSpec((1, 6, D_MODEL), cond),
            _full(fw.shape), _full(fb.shape), _full(lng.shape), _full(lnb.shape),
        ],
        out_specs=(pl.BlockSpec((tk, D_MODEL), out_row), pl.BlockSpec((tk, D_MODEL), out_row)),
        scratch_shapes=[pltpu.VMEM((tk, D_MODEL), F32)],
        compiler_params=_params("parallel", "parallel", "arbitrary"),
        name="fnet_seq",
    )(cl, sl, hc, hs, x, mod, fw, fb, lng, lnb)


def _top16(x):
    rows = []
    for _ in range(PEER_TOPK):
        m = jnp.max(x, axis=0, keepdims=True)
        rows.append(m)
        x = jnp.where(x == m, -jnp.inf, x)
    return rows


def _stack8(rows, row8):
    out = jnp.broadcast_to(rows[0], row8.shape)
    for k in range(1, 8):
        out = jnp.where(row8 == k, rows[k], out)
    return out


def _peer_topk_kernel(h_ref, wqt_ref, sk_ref, a2_ref, e2_ref, th_ref, e1_ref, qt_ref, *, tile):
    qt_ref[...] = _nt(wqt_ref[...], h_ref[...]).astype(qt_ref.dtype)
    row8 = lax.broadcasted_iota(jnp.int32, (8, LANES), 0)

    def head(hd, carry):
        base = pl.multiple_of(hd * PEER_QUERY_DIM, PEER_QUERY_DIM)
        for lg in range(tile // LANES):
            ls = slice(lg * LANES, (lg + 1) * LANES)
            s1 = _nn(sk_ref[2 * hd], qt_ref[pl.ds(base, PEER_HALF), ls])
            s2 = _nn(sk_ref[2 * hd + 1], qt_ref[pl.ds(base + PEER_HALF, PEER_HALF), ls])
            t1 = _top16(s1)
            t2 = _top16(s2)
            t2a = _stack8(t2[:8], row8)
            t2b = _stack8(t2[8:], row8)
            t1b = _stack8(t1[8:], row8)
            cands = [t1[0] + t2a, t1[0] + t2b]
            for k in range(1, 8):
                cands.append(jnp.where(row8 < PEER_TOPK // (k + 1), t1[k] + t2a, -jnp.inf))
            cands.append(t1b + t2[0])
            c = jnp.concatenate(cands, axis=0)
            top = t1[0] + t2[0]
            z = jnp.zeros_like(top)
            tau = top
            for _ in range(PEER_TOPK):
                tau = jnp.max(c, axis=0, keepdims=True)
                z = z + jnp.exp(tau - top)
                c = jnp.where(c == tau, -jnp.inf, c)
            theta = jnp.full_like(s1, jnp.inf)
            for l in range(PEER_TOPK):
                theta = jnp.where(s1 + t2[l] >= tau, t2[l], theta)
            a2_ref[hd, lg] = s2
            e2_ref[hd, lg] = jnp.exp(s2 - t2[0]) * (0.5 / z)
            th_ref[hd, lg] = theta
            e1_ref[hd, lg] = jnp.exp(s1 - t1[0])
        return carry

    lax.fori_loop(0, PEER_HEADS, head, 0)


def _peer_topk(h2, wqt, sk, tile):
    n = h2.shape[0]
    g = tile // LANES
    shp = jax.ShapeDtypeStruct((PEER_HEADS, n // LANES, PEER_N_KEYS, LANES), F32)
    spec = pl.BlockSpec((PEER_HEADS, g, PEER_N_KEYS, LANES), lambda i: (0, i, 0, 0))
    return pl.pallas_call(
        functools.partial(_peer_topk_kernel, tile=tile),
        out_shape=(shp, shp, shp, shp),
        grid=(n // tile,),
        in_specs=[pl.BlockSpec((tile, D_MODEL), lambda i: (i, 0)), _full(wqt.shape), _full(sk.shape)],
        out_specs=(spec, spec, spec, spec),
        scratch_shapes=[pltpu.VMEM((PEER_HEADS * PEER_QUERY_DIM, tile), MM_DTYPE)],
        compiler_params=_params("parallel"),
        name="peer_topk",
    )(h2, wqt, sk)


def _peer_main_kernel(h_ref, u_ref, vt_ref, a2_ref, e2_ref, th_ref, e1_ref, x1_ref, mod_ref, lng_ref, lnb_ref,
                      out_ref, at_ref, hm_ref, acc_ref, *, tile, eblk):
    j = pl.program_id(1)

    @pl.when(j == 0)
    def _():
        acc_ref[...] = jnp.zeros_like(acc_ref)

    at_ref[...] = _nt(u_ref[...], h_ref[...])
    for lg in range(tile // LANES):
        ls = slice(lg * LANES, (lg + 1) * LANES)
        for il in range(eblk // PEER_N_KEYS):
            rs = slice(il * PEER_N_KEYS, (il + 1) * PEER_N_KEYS)
            gate = None
            for hd in range(PEER_HEADS):
                w = jnp.where(a2_ref[hd, lg] >= th_ref[hd, lg, il:il + 1, :],
                              e1_ref[hd, lg, il:il + 1, :] * e2_ref[hd, lg], 0.0)
                gate = w if gate is None else gate + w
            a = at_ref[rs, ls]
            hm_ref[rs, ls] = (gate * a * (1.0 + lax.erf(a * (2.0 ** -0.5)))).astype(hm_ref.dtype)
    acc_ref[...] += _nn(vt_ref[...], hm_ref[...])

    @pl.when(j == pl.num_programs(1) - 1)
    def _():
        f = acc_ref[...].T
        z = DEEPNORM_ALPHA * x1_ref[...] + mod_ref[0, 5:6, :] * f
        out_ref[...] = _layer_norm(z, lng_ref[...], lnb_ref[...])


def _peer_main(h2, u, vt, a2, e2, th, e1, x1, mod, cond_fn, lng, lnb, tile, eblk):
    n = h2.shape[0]
    g = tile // LANES
    ik = eblk // PEER_N_KEYS
    row = lambda i, j: (i, 0)
    full_keys = pl.BlockSpec((PEER_HEADS, g, PEER_N_KEYS, LANES), lambda i, j: (0, i, 0, 0))
    slab_keys = pl.BlockSpec((PEER_HEADS, g, ik, LANES), lambda i, j: (0, i, j, 0))
    return pl.pallas_call(
        functools.partial(_peer_main_kernel, tile=tile, eblk=eblk),
        out_shape=jax.ShapeDtypeStruct((n, D_MODEL), F32),
        grid=(n // tile, PEER_N_EXPERTS // eblk),
        in_specs=[
            pl.BlockSpec((tile, D_MODEL), row),
            pl.BlockSpec((eblk, D_MODEL), lambda i, j: (j, 0)),
            pl.BlockSpec((D_MODEL, eblk), lambda i, j: (0, j)),
            full_keys, full_keys, slab_keys, slab_keys,
            pl.BlockSpec((tile, D_MODEL), row),
            pl.BlockSpec((1, 6, D_MODEL), lambda i, j: (cond_fn(i), 0, 0)),
            pl.BlockSpec(lng.shape, lambda i, j: (0, 0)), pl.BlockSpec(lnb.shape, lambda i, j: (0, 0)),
        ],
        out_specs=pl.BlockSpec((tile, D_MODEL), row),
        scratch_shapes=[pltpu.VMEM((eblk, tile), F32), pltpu.VMEM((eblk, tile), MM_DTYPE),
                        pltpu.VMEM((D_MODEL, tile), F32)],
        compiler_params=_params("parallel", "arbitrary"),
        name="peer_main",
    )(h2, u, vt, a2, e2, th, e1, x1, mod, lng, lnb)


def _peer_layer(h2, x1, mod, cond_fn, pw, lng, lnb, tile_topk, tile_main, eblk):
    a2, e2, th, e1 = _peer_topk(h2, pw["wqt"], pw["sk"], tile_topk)
    return _peer_main(h2, pw["u"], pw["vt"], a2, e2, th, e1, x1, mod, cond_fn, lng, lnb, tile_main, eblk)


def _pad_heads(w, n_used):
    k = w.shape[0]
    return jnp.pad(w, ((0, 0), (0, 0), (0, HEAD_PAD - n_used))).reshape(k, MLA_HEADS * HEAD_PAD)


def _rotate_half_cols(w):
    n = QK_ROPE_DIM // 4
    w4 = w.reshape(w.shape[:-1] + (2, 2, n))
    return jnp.stack([-w4[..., 1, :], w4[..., 0, :]], axis=-2).reshape(w.shape)


def _mla_weights(wdq, gq, wuq, wdkv, gkv, wukv):
    dq = QK_NOPE_DIM + QK_ROPE_DIM
    wuq3 = wuq.reshape(Q_LORA_RANK, MLA_HEADS, dq)
    pe = wuq3[..., QK_NOPE_DIM:]
    zeros_nope = jnp.zeros((Q_LORA_RANK, MLA_HEADS, QK_NOPE_DIM), F32)
    wuq_pad = _pad_heads(wuq3, dq)
    wuqs_pad = _pad_heads(jnp.concatenate([zeros_nope, _rotate_half_cols(pe)], axis=-1), dq)
    wkpe = wdkv[:, KV_LORA_RANK:]
    zk = jnp.zeros((D_MODEL, ROPE_OFF), F32)
    zt = jnp.zeros((D_MODEL, HEAD_PAD - ROPE_OFF - QK_ROPE_DIM), F32)
    wdkv_pad = jnp.concatenate([wdkv[:, :KV_LORA_RANK], zk, wkpe, zt, zk, _rotate_half_cols(wkpe), zt], axis=-1)
    wukv3 = wukv.reshape(KV_LORA_RANK, MLA_HEADS, QK_NOPE_DIM + V_HEAD_DIM)
    place = jnp.pad(jnp.eye(QK_ROPE_DIM, dtype=F32), ((0, 0), (ROPE_OFF, HEAD_PAD - ROPE_OFF - QK_ROPE_DIM)))
    return dict(
        wdq=wdq.astype(MM_DTYPE), gq=gq.reshape(1, -1), wuq=wuq_pad.astype(MM_DTYPE), wuqs=wuqs_pad.astype(MM_DTYPE),
        wdkv=wdkv_pad.astype(MM_DTYPE), gkv=gkv.reshape(1, -1),
        wukvk=_pad_heads(wukv3[..., :QK_NOPE_DIM], QK_NOPE_DIM).astype(MM_DTYPE),
        wukvv=wukv3[..., QK_NOPE_DIM:].reshape(KV_LORA_RANK, MLA_HEADS * V_HEAD_DIM).astype(MM_DTYPE),
        place=place.astype(MM_DTYPE),
    )


def _rope_tables(length):
    rows = length // GRID_W
    t = jnp.arange(rows * GRID_W)
    row = (t // GRID_W).astype(F32)
    col = (t % GRID_W).astype(F32)
    half = QK_ROPE_DIM // 2
    inv_freq = 1.0 / jnp.power(ROPE_THETA, jnp.arange(0, half, 2, dtype=F32) / half)
    n = QK_ROPE_DIM // 4
    ang = jnp.stack([row[:, None] * inv_freq, col[:, None] * inv_freq], axis=1)
    ang = jnp.broadcast_to(ang[:, :, None, :], (length, 2, 2, n)).reshape(length, QK_ROPE_DIM)
    pad = ((0, 0), (ROPE_OFF, HEAD_PAD - ROPE_OFF - QK_ROPE_DIM))
    cos = jnp.pad(jnp.cos(ang), pad, constant_values=1.0)
    sin = jnp.pad(jnp.sin(ang), pad)
    return cos, sin


def _dft_tables(length, scale):
    k = jnp.arange(length, dtype=jnp.int32)
    kt = (k[:, None] * k[None, :]) % length
    ang = kt.astype(F32) * (2.0 * math.pi / length)
    return (jnp.cos(ang) * scale).astype(MM_DTYPE), (jnp.sin(ang) * scale).astype(MM_DTYPE)


def kernel(x_prompt, x_sample, cache_ckv, cache_kpe, c, c_ctx, ada_w, ada_b, ln_mix_g, ln_mix_b, ln_ffn_g, ln_ffn_b, mla_wdq, mla_q_norm_g, mla_wuq, mla_wdkv, mla_kv_norm_g, mla_wukv, mla_wo, pool_w, pool_scale, fnet_w, fnet_b, peer_wq, peer_subkeys, peer_u, peer_v):
    nb_p, len_p, _ = x_prompt.shape
    nb_s, len_s, _ = x_sample.shape
    assert 1 + nb_s <= N_COND_ROWS
    cond = jnp.zeros((N_COND_ROWS, D_MODEL), F32).at[0].set(c_ctx).at[1:1 + nb_s].set(c)
    mods = _adaln(cond, ada_w, ada_b)

    streams = [
        dict(x=x_prompt.reshape(nb_p * len_p, D_MODEL), n_seq=nb_p, seq_len=len_p, per_seq=False),
        dict(x=x_sample.reshape(nb_s * len_s, D_MODEL), n_seq=nb_s, seq_len=len_s, per_seq=True),
    ]
    tile = 256
    tile_main = 512
    eblk = 1024
    cos_s, sin_s = _rope_tables(len_s)
    ident_cos = jnp.ones((tile, HEAD_PAD), F32)
    ident_sin = jnp.zeros((tile, HEAD_PAD), F32)
    new_ckv, new_kpe = [], []

    for i in range(DEPTH):
        kind, j = i % N_MIXERS, i // N_MIXERS
        mod = mods[i]
        lng, lnb = ln_mix_g[i].reshape(1, -1), ln_mix_b[i].reshape(1, -1)
        pw = dict(
            wqt=peer_wq[i].T.astype(MM_DTYPE),
            sk=peer_subkeys[i].reshape(PEER_HEADS * 2, PEER_N_KEYS, PEER_HALF).astype(MM_DTYPE),
            u=peer_u[i].astype(MM_DTYPE),
            vt=peer_v[i].T.astype(MM_DTYPE),
        )
        if kind == 0:
            mw = _mla_weights(mla_wdq[j], mla_q_norm_g[j], mla_wuq[j], mla_wdkv[j], mla_kv_norm_g[j], mla_wukv[j])
            wo = mla_wo[j].astype(MM_DTYPE)
            ctx = _mla_ctx(cache_ckv, cache_kpe, j, mw)
        elif kind == 2:
            cc, sc = _dft_tables(FOURIER_GROUP_DIM, FOURIER_GROUP_DIM ** -0.5)
        for s in streams:
            x, seq_len, n_seq = s["x"], s["seq_len"], s["n_seq"]
            cond_fn = _cond_fn(s["per_seq"], tile, seq_len)
            if kind == 0:
                if s["per_seq"]:
                    tps = seq_len // tile
                    q, k, v, _, _ = _mla_proj(x, mod, cond_fn, cos_s, sin_s, lambda t, tps=tps: t % tps, mw, tile)
                    o = _attention(q, k, v, n_seq, seq_len, min(256, seq_len), ctx=ctx)
                else:
                    q, k, v, ckv, kpe = _mla_proj(x, mod, cond_fn, ident_cos, ident_sin, lambda t: 0, mw, tile)
                    o = _attention(q, k, v, n_seq, seq_len, min(256, seq_len))
                    new_ckv.append(ckv.reshape(n_seq, seq_len, KV_LORA_RANK))
                    new_kpe.append(kpe.reshape(n_seq, seq_len, QK_ROPE_DIM))
                x1, h2 = _mla_post(o, x, mod, cond_fn, wo, lng, lnb, tile)
            elif kind == 1:
                x1, h2 = _pool_layer(x, mod, cond_fn, seq_len, pool_w[j].astype(MM_DTYPE),
                                     pool_scale[j].reshape(1, -1), lng, lnb, tile)
            else:
                hc, hs = _fnet_chan(x, mod, cond_fn, cc, sc, tile)
                cl, sl = _dft_tables(seq_len, seq_len ** -0.5)
                x1, h2 = _fnet_seq(hc, hs, x, mod, s["per_seq"], n_seq, seq_len, cl, sl,
                                   fnet_w[j].astype(MM_DTYPE), fnet_b[j].reshape(1, -1), lng, lnb,
                                   min(512, seq_len), min(1024, seq_len))
            cond_main = _cond_fn(s["per_seq"], min(tile_main, x.shape[0]), seq_len)
            s["x"] = _peer_layer(h2, x1, mod, cond_main, pw, ln_ffn_g[i].reshape(1, -1), ln_ffn_b[i].reshape(1, -1),
                                 tile, min(tile_main, x.shape[0]), eblk)

    y_prompt = streams[0]["x"].reshape(nb_p, len_p, D_MODEL)
    y_sample = streams[1]["x"].reshape(nb_s, len_s, D_MODEL)
    return (y_prompt, y_sample, jnp.stack(new_ckv, axis=1), jnp.stack(new_kpe, axis=1))
```

```python
import functools
import math

import jax
import jax.numpy as jnp
from jax import lax
from jax.experimental import pallas as pl
from jax.experimental.pallas import tpu as pltpu

F32 = jnp.float32
MM_DTYPE = jnp.bfloat16

D_MODEL = 1024
DEPTH = 4
GRID_W = 64
N_MIXERS = 3

MLA_HEADS = 16
QK_NOPE_DIM = 64
QK_ROPE_DIM = 32
V_HEAD_DIM = 64
Q_LORA_RANK = 384
KV_LORA_RANK = 256
ROPE_THETA = 10000.0
HEAD_PAD = 128
ROPE_OFF = QK_NOPE_DIM

POOL_WINDOWS = (2, 4, 8, 16)
N_POOL_GROUPS = 4
POOL_GROUP_DIM = D_MODEL // N_POOL_GROUPS
POOL_HALO = 8

N_FOURIER_GROUPS = 4
FOURIER_GROUP_DIM = D_MODEL // N_FOURIER_GROUPS

PEER_HEADS = 8
PEER_N_KEYS = 128
PEER_N_EXPERTS = PEER_N_KEYS * PEER_N_KEYS
PEER_QUERY_DIM = 256
PEER_HALF = PEER_QUERY_DIM // 2
PEER_TOPK = 16

DEEPNORM_ALPHA = (2 * DEPTH) ** 0.25
LN_EPS = 1e-5
RMS_EPS = 1e-6

LANES = 128
SUBLANES = 8
N_COND_ROWS = 16
VMEM_LIMIT = 56 * 1024 * 1024


def _nn(a, b):
    return jnp.dot(a, b, preferred_element_type=F32)


def _nt(a, b):
    return lax.dot_general(a, b, (((1,), (1,)), ((), ())), preferred_element_type=F32)


def _params(*sem, flags=None):
    return pltpu.CompilerParams(dimension_semantics=sem, vmem_limit_bytes=VMEM_LIMIT, flags=flags)


def _layer_norm(z, g, b):
    mu = jnp.mean(z, axis=-1, keepdims=True)
    d = z - mu
    var = jnp.mean(d * d, axis=-1, keepdims=True)
    return d * lax.rsqrt(var + LN_EPS) * g + b


def _rms_norm(x, g):
    return x * lax.rsqrt(jnp.mean(x * x, axis=-1, keepdims=True) + RMS_EPS) * g


def _full(shape):
    n = len(shape)
    return pl.BlockSpec(shape, lambda *_: (0,) * n)


def _mod_spec(cond_fn):
    return pl.BlockSpec((1, 6, D_MODEL), lambda i, *_: (cond_fn(i), 0, 0))


def _cond_fn(per_seq, tile, seq_len):
    if not per_seq:
        return lambda i: 0
    return lambda i: 1 + (i * tile) // seq_len


def _adaln_kernel(cond_ref, w_ref, b_ref, o_ref):
    c = cond_ref[...]
    c = c / (1.0 + jnp.exp(-c))
    o_ref[0] = jnp.dot(c, w_ref[0], preferred_element_type=F32,
                       precision=lax.Precision.HIGHEST) + b_ref[0]


def _adaln(cond, ada_w, ada_b):
    nb = 1536
    out = pl.pallas_call(
        _adaln_kernel,
        out_shape=jax.ShapeDtypeStruct((DEPTH, N_COND_ROWS, 6 * D_MODEL), F32),
        grid=(DEPTH, 6 * D_MODEL // nb),
        in_specs=[
            pl.BlockSpec((N_COND_ROWS, D_MODEL), lambda i, j: (0, 0)),
            pl.BlockSpec((1, D_MODEL, nb), lambda i, j: (i, 0, j)),
            pl.BlockSpec((1, 1, nb), lambda i, j: (i, 0, j)),
        ],
        out_specs=pl.BlockSpec((1, N_COND_ROWS, nb), lambda i, j: (i, 0, j)),
        compiler_params=_params("parallel", "parallel"),
        name="adaln",
    )(cond, ada_w, ada_b.reshape(DEPTH, 1, 6 * D_MODEL))
    return out.reshape(DEPTH, N_COND_ROWS, 6, D_MODEL)


def _mla_proj_kernel(x_ref, mod_ref, cos_ref, sin_ref, wdq_ref, gq_ref, wuq_ref, wuqs_ref,
                     wdkv_ref, gkv_ref, wukvk_ref, wukvv_ref,
                     q_ref, k_ref, v_ref, ckv_ref, kpe_ref):
    h = x_ref[...] * (1.0 + mod_ref[0, 1:2, :]) + mod_ref[0, 0:1, :]
    hb = h.astype(MM_DTYPE)
    cq = _rms_norm(_nn(hb, wdq_ref[...]), gq_ref[...]).astype(MM_DTYPE)
    q = _nn(cq, wuq_ref[...])
    qs = _nn(cq, wuqs_ref[...])
    kv = _nn(hb, wdkv_ref[...])
    ckv = _rms_norm(kv[:, :KV_LORA_RANK], gkv_ref[...])
    kpe = kv[:, KV_LORA_RANK:KV_LORA_RANK + HEAD_PAD]
    kpes = kv[:, KV_LORA_RANK + HEAD_PAD:]
    cos = cos_ref[...]
    sin = sin_ref[...]
    kpe_rot = kpe * cos + kpes * sin
    ckvb = ckv.astype(MM_DTYPE)
    kn = _nn(ckvb, wukvk_ref[...])
    scale = (QK_NOPE_DIM + QK_ROPE_DIM) ** -0.5
    for hd in range(MLA_HEADS):
        sl = slice(hd * HEAD_PAD, (hd + 1) * HEAD_PAD)
        q_ref[:, sl] = ((q[:, sl] * cos + qs[:, sl] * sin) * scale).astype(q_ref.dtype)
        k_ref[:, sl] = (kn[:, sl] + kpe_rot).astype(k_ref.dtype)
    v_ref[...] = _nn(ckvb, wukvv_ref[...]).astype(v_ref.dtype)
    ckv_ref[...] = ckv
    kpe_ref[...] = kpe[:, ROPE_OFF:ROPE_OFF + QK_ROPE_DIM]


def _mla_proj(x, mod, cond_fn, cos, sin, pos_fn, w, tile):
    n = x.shape[0]
    hp = MLA_HEADS * HEAD_PAD
    hv = MLA_HEADS * V_HEAD_DIM
    row = lambda i: (i, 0)
    return pl.pallas_call(
        _mla_proj_kernel,
        out_shape=(
            jax.ShapeDtypeStruct((n, hp), MM_DTYPE),
            jax.ShapeDtypeStruct((n, hp), MM_DTYPE),
            jax.ShapeDtypeStruct((n, hv), MM_DTYPE),
            jax.ShapeDtypeStruct((n, KV_LORA_RANK), F32),
            jax.ShapeDtypeStruct((n, QK_ROPE_DIM), F32),
        ),
        grid=(n // tile,),
        in_specs=[
            pl.BlockSpec((tile, D_MODEL), row),
            _mod_spec(cond_fn),
            pl.BlockSpec((tile, HEAD_PAD), lambda i: (pos_fn(i), 0)),
            pl.BlockSpec((tile, HEAD_PAD), lambda i: (pos_fn(i), 0)),
            _full(w["wdq"].shape), _full(w["gq"].shape), _full(w["wuq"].shape), _full(w["wuqs"].shape),
            _full(w["wdkv"].shape), _full(w["gkv"].shape), _full(w["wukvk"].shape), _full(w["wukvv"].shape),
        ],
        out_specs=(
            pl.BlockSpec((tile, hp), row),
            pl.BlockSpec((tile, hp), row),
            pl.BlockSpec((tile, hv), row),
            pl.BlockSpec((tile, KV_LORA_RANK), row),
            pl.BlockSpec((tile, QK_ROPE_DIM), row),
        ),
        compiler_params=_params("parallel"),
        name="mla_proj",
    )(x, mod, cos, sin, w["wdq"], w["gq"], w["wuq"], w["wuqs"], w["wdkv"], w["gkv"], w["wukvk"], w["wukvv"])


def _mla_ctx_kernel(ckv_ref, kpe_ref, wukvk_ref, wukvv_ref, place_ref, k_ref, v_ref):
    ckvb = ckv_ref[...].astype(MM_DTYPE)
    kn = _nn(ckvb, wukvk_ref[...])
    kpe = _nn(kpe_ref[...].astype(MM_DTYPE), place_ref[...])
    for hd in range(MLA_HEADS):
        sl = slice(hd * HEAD_PAD, (hd + 1) * HEAD_PAD)
        k_ref[:, sl] = (kn[:, sl] + kpe).astype(k_ref.dtype)
    v_ref[...] = _nn(ckvb, wukvv_ref[...]).astype(v_ref.dtype)


def _mla_ctx(cache_ckv, cache_kpe, j, w):
    nb, _, past, _ = cache_ckv.shape
    hp = MLA_HEADS * HEAD_PAD
    hv = MLA_HEADS * V_HEAD_DIM
    return pl.pallas_call(
        _mla_ctx_kernel,
        out_shape=(jax.ShapeDtypeStruct((nb * past, hp), MM_DTYPE),
                   jax.ShapeDtypeStruct((nb * past, hv), MM_DTYPE)),
        grid=(nb,),
        in_specs=[
            pl.BlockSpec((None, None, past, KV_LORA_RANK), lambda b: (b, j, 0, 0)),
            pl.BlockSpec((None, None, past, QK_ROPE_DIM), lambda b: (b, j, 0, 0)),
            _full(w["wukvk"].shape), _full(w["wukvv"].shape), _full(w["place"].shape),
        ],
        out_specs=(pl.BlockSpec((past, hp), lambda b: (b, 0)),
                   pl.BlockSpec((past, hv), lambda b: (b, 0))),
        compiler_params=_params("parallel"),
        name="mla_ctx",
    )(cache_ckv, cache_kpe, w["wukvk"], w["wukvv"], w["place"])


def _attn_kernel(*refs, has_ctx):
    if has_ctx:
        q_ref, k_ref, v_ref, kc_ref, vc_ref, o_ref = refs
    else:
        q_ref, k_ref, v_ref, o_ref = refs
    lane = lax.broadcasted_iota(jnp.int32, (1, 2 * V_HEAD_DIM), 1)
    out = None
    for hh in range(2):
        sl = slice(hh * HEAD_PAD, (hh + 1) * HEAD_PAD)
        own = (lane >= hh * V_HEAD_DIM) & (lane < (hh + 1) * V_HEAD_DIM)
        q = q_ref[:, sl]
        s = _nt(q, k_ref[:, sl])
        m = jnp.max(s, axis=-1, keepdims=True)
        if has_ctx:
            sc = _nt(q, kc_ref[:, sl])
            m = jnp.maximum(m, jnp.max(sc, axis=-1, keepdims=True))
        p = jnp.exp(s - m)
        l = jnp.sum(p, axis=-1, keepdims=True)
        v = v_ref[...]
        o = _nn(p.astype(MM_DTYPE), jnp.where(own, v, jnp.zeros_like(v)))
        if has_ctx:
            pc = jnp.exp(sc - m)
            l = l + jnp.sum(pc, axis=-1, keepdims=True)
            vc = vc_ref[...]
            o = o + _nn(pc.astype(MM_DTYPE), jnp.where(own, vc, jnp.zeros_like(vc)))
        o = o * (1.0 / l)
        out = o if out is None else out + o
    o_ref[...] = out.astype(o_ref.dtype)


def _attention(q, k, v, n_seq, seq_len, tq, ctx=None):
    n = q.shape[0]
    hpairs = MLA_HEADS // 2
    qb = seq_len // tq
    in_specs = [
        pl.BlockSpec((tq, 2 * HEAD_PAD), lambda b, h, i: (b * qb + i, h)),
        pl.BlockSpec((seq_len, 2 * HEAD_PAD), lambda b, h, i: (b, h)),
        pl.BlockSpec((seq_len, 2 * V_HEAD_DIM), lambda b, h, i: (b, h)),
    ]
    args = [q, k, v]
    if ctx is not None:
        kc, vc = ctx
        past = kc.shape[0] // n_seq
        in_specs += [
            pl.BlockSpec((past, 2 * HEAD_PAD), lambda b, h, i: (b, h)),
            pl.BlockSpec((past, 2 * V_HEAD_DIM), lambda b, h, i: (b, h)),
        ]
        args += [kc, vc]
    return pl.pallas_call(
        functools.partial(_attn_kernel, has_ctx=ctx is not None),
        out_shape=jax.ShapeDtypeStruct((n, MLA_HEADS * V_HEAD_DIM), MM_DTYPE),
        grid=(n_seq, hpairs, qb),
        in_specs=in_specs,
        out_specs=pl.BlockSpec((tq, 2 * V_HEAD_DIM), lambda b, h, i: (b * qb + i, h)),
        compiler_params=_params("parallel", "parallel", "arbitrary"),
        name="mla_attn",
    )(*args)


def _post_store(z, mod_ref, lng_ref, lnb_ref, x1_ref, h2_ref):
    x1 = _layer_norm(z, lng_ref[...], lnb_ref[...])
    x1_ref[...] = x1
    h2_ref[...] = (x1 * (1.0 + mod_ref[0, 4:5, :]) + mod_ref[0, 3:4, :]).astype(h2_ref.dtype)


def _mla_post_kernel(o_ref, x_ref, mod_ref, wo_ref, lng_ref, lnb_ref, x1_ref, h2_ref):
    y = _nn(o_ref[...], wo_ref[...])
    z = DEEPNORM_ALPHA * x_ref[...] + mod_ref[0, 2:3, :] * y
    _post_store(z, mod_ref, lng_ref, lnb_ref, x1_ref, h2_ref)


def _mla_post(o, x, mod, cond_fn, wo, lng, lnb, tile):
    n = x.shape[0]
    row = lambda i: (i, 0)
    return pl.pallas_call(
        _mla_post_kernel,
        out_shape=(jax.ShapeDtypeStruct((n, D_MODEL), F32), jax.ShapeDtypeStruct((n, D_MODEL), MM_DTYPE)),
        grid=(n // tile,),
        in_specs=[
            pl.BlockSpec((tile, D_MODEL), row), pl.BlockSpec((tile, D_MODEL), row), _mod_spec(cond_fn),
            _full(wo.shape), _full(lng.shape), _full(lnb.shape),
        ],
        out_specs=(pl.BlockSpec((tile, D_MODEL), row), pl.BlockSpec((tile, D_MODEL), row)),
        compiler_params=_params("parallel"),
        name="mla_post",
    )(o, x, mod, wo, lng, lnb)


def _pool_kernel(x_ref, xp_ref, xn_ref, mod_ref, w_ref, ps_ref, lng_ref, lnb_ref, x1_ref, h2_ref, ext_ref,
                 *, tile, tiles_per_seq):
    i = pl.program_id(0)
    pos = i % tiles_per_seq
    scale1 = 1.0 + mod_ref[0, 1:2, :]
    shift1 = mod_ref[0, 0:1, :]
    x = x_ref[...]
    keep_prev = (pos > 0).astype(F32)
    keep_next = (pos < tiles_per_seq - 1).astype(F32)
    ext_ref[0:POOL_HALO, :] = (xp_ref[...] * scale1 + shift1) * keep_prev
    ext_ref[POOL_HALO:POOL_HALO + tile, :] = x * scale1 + shift1
    ext_ref[POOL_HALO + tile:, :] = (xn_ref[...] * scale1 + shift1) * keep_next
    seq_len = tile * tiles_per_seq
    t = pos * tile + lax.broadcasted_iota(jnp.int32, (tile, POOL_GROUP_DIM), 0)
    ys = []
    for g, w in enumerate(POOL_WINDOWS):
        cs = slice(g * POOL_GROUP_DIM, (g + 1) * POOL_GROUP_DIM)
        tot = None
        for d in range(-(w // 2), w - w // 2):
            piece = ext_ref[POOL_HALO + d:POOL_HALO + d + tile, cs]
            tot = piece if tot is None else tot + piece
        lo = jnp.maximum(t - w // 2, 0)
        hi = jnp.minimum(t + (w - w // 2) - 1, seq_len - 1)
        cnt = (hi - lo + 1).astype(F32)
        diff = tot / cnt - ext_ref[POOL_HALO:POOL_HALO + tile, cs]
        ys.append(_nn(diff.astype(MM_DTYPE), w_ref[g]))
    y = jnp.concatenate(ys, axis=-1) * ps_ref[...]
    z = DEEPNORM_ALPHA * x + mod_ref[0, 2:3, :] * y
    _post_store(z, mod_ref, lng_ref, lnb_ref, x1_ref, h2_ref)


def _pool_layer(x, mod, cond_fn, seq_len, w, ps, lng, lnb, tile):
    n = x.shape[0]
    tps = seq_len // tile
    hb = tile // POOL_HALO
    nblk = n // POOL_HALO
    row = lambda i: (i, 0)
    return pl.pallas_call(
        functools.partial(_pool_kernel, tile=tile, tiles_per_seq=tps),
        out_shape=(jax.ShapeDtypeStruct((n, D_MODEL), F32), jax.ShapeDtypeStruct((n, D_MODEL), MM_DTYPE)),
        grid=(n // tile,),
        in_specs=[
            pl.BlockSpec((tile, D_MODEL), row),
            pl.BlockSpec((POOL_HALO, D_MODEL), lambda i: (jnp.maximum(i * hb - 1, 0), 0)),
            pl.BlockSpec((POOL_HALO, D_MODEL), lambda i: (jnp.minimum((i + 1) * hb, nblk - 1), 0)),
            _mod_spec(cond_fn), _full(w.shape), _full(ps.shape), _full(lng.shape), _full(lnb.shape),
        ],
        out_specs=(pl.BlockSpec((tile, D_MODEL), row), pl.BlockSpec((tile, D_MODEL), row)),
        scratch_shapes=[pltpu.VMEM((tile + 2 * POOL_HALO, D_MODEL), F32)],
        compiler_params=_params("parallel"),
        name="pool_layer",
    )(x, x, x, mod, w, ps, lng, lnb)


def _fnet_chan_kernel(x_ref, mod_ref, cc_ref, sc_ref, hc_ref, hs_ref):
    h = x_ref[...] * (1.0 + mod_ref[0, 1:2, :]) + mod_ref[0, 0:1, :]
    hb = h.astype(MM_DTYPE)
    for g in range(N_FOURIER_GROUPS):
        cs = slice(g * FOURIER_GROUP_DIM, (g + 1) * FOURIER_GROUP_DIM)
        hc_ref[:, cs] = _nn(hb[:, cs], cc_ref[...]).astype(hc_ref.dtype)
        hs_ref[:, cs] = _nn(hb[:, cs], sc_ref[...]).astype(hs_ref.dtype)


def _fnet_chan(x, mod, cond_fn, cc, sc, tile):
    n = x.shape[0]
    row = lambda i: (i, 0)
    return pl.pallas_call(
        _fnet_chan_kernel,
        out_shape=(jax.ShapeDtypeStruct((n, D_MODEL), MM_DTYPE), jax.ShapeDtypeStruct((n, D_MODEL), MM_DTYPE)),
        grid=(n // tile,),
        in_specs=[pl.BlockSpec((tile, D_MODEL), row), _mod_spec(cond_fn), _full(cc.shape), _full(sc.shape)],
        out_specs=(pl.BlockSpec((tile, D_MODEL), row), pl.BlockSpec((tile, D_MODEL), row)),
        compiler_params=_params("parallel"),
        name="fnet_chan",
    )(x, mod, cc, sc)


def _fnet_seq_kernel(cl_ref, sl_ref, hc_ref, hs_ref, x_ref, mod_ref, fw_ref, fb_ref, lng_ref, lnb_ref,
                     x1_ref, h2_ref, acc_ref):
    tt = pl.program_id(2)

    @pl.when(tt == 0)
    def _():
        acc_ref[...] = jnp.zeros_like(acc_ref)

    acc_ref[...] += _nn(cl_ref[...], hc_ref[...]) - _nn(sl_ref[...], hs_ref[...])

    @pl.when(tt == pl.num_programs(2) - 1)
    def _():
        y = _nn(acc_ref[...].astype(MM_DTYPE), fw_ref[...]) + fb_ref[...]
        z = DEEPNORM_ALPHA * x_ref[...] + mod_ref[0, 2:3, :] * y
        _post_store(z, mod_ref, lng_ref, lnb_ref, x1_ref, h2_ref)


def _fnet_seq(hc, hs, x, mod, per_seq, n_seq, seq_len, cl, sl, fw, fb, lng, lnb, tk, tt):
    n = x.shape[0]
    kb = seq_len // tk
    tb = seq_len // tt
    out_row = lambda b, k, t: (b * kb + k, 0)
    cond = (lambda b, k, t: (1 + b, 0, 0)) if per_seq else (lambda b, k, t: (0, 0, 0))
    return pl.pallas_call(
        _fnet_seq_kernel,
        out_shape=(jax.ShapeDtypeStruct((n, D_MODEL), F32), jax.ShapeDtypeStruct((n, D_MODEL), MM_DTYPE)),
        grid=(n_seq, kb, tb),
        in_specs=[
            pl.BlockSpec((tk, tt), lambda b, k, t: (k, t)),
            pl.BlockSpec((tk, tt), lambda b, k, t: (k, t)),
            pl.BlockSpec((tt, D_MODEL), lambda b, k, t: (b * tb + t, 0)),
            pl.BlockSpec((tt, D_MODEL), lambda b, k, t: (b * tb + t, 0)),
            pl.BlockSpec((tk, D_MODEL), out_row),
            pl.BlockSpec((1, 6, D_MODEL), cond),
            _full(fw.shape), _full(fb.shape), _full(lng.shape), _full(lnb.shape),
        ],
        out_specs=(pl.BlockSpec((tk, D_MODEL), out_row), pl.BlockSpec((tk, D_MODEL), out_row)),
        scratch_shapes=[pltpu.VMEM((tk, D_MODEL), F32)],
        compiler_params=_params("parallel", "parallel", "arbitrary"),
        name="fnet_seq",
    )(cl, sl, hc, hs, x, mod, fw, fb, lng, lnb)


def _top16(x):
    v = _sort_desc([x[SUBLANES * r:SUBLANES * (r + 1)] for r in range(x.shape[0] // SUBLANES)])
    rows = []
    for k in range(PEER_TOPK):
        m = jnp.max(v[0], axis=0, keepdims=True)
        rows.append(m)
        hit = v[0] == m
        v = [jnp.where(hit, v[d + 1], v[d]) for d in range(PEER_TOPK - 1 - k)]
    return rows


def _sort_desc(v):
    v = list(v)
    n = len(v)
    p = 1
    while p < n:
        k = p
        while k >= 1:
            for j in range(k % p, n - k, 2 * k):
                for i in range(min(k, n - j - k)):
                    if (i + j) // (2 * p) == (i + j + k) // (2 * p):
                        a, b = v[i + j], v[i + j + k]
                        v[i + j], v[i + j + k] = jnp.maximum(a, b), jnp.minimum(a, b)
            k //= 2
        p *= 2
    return v


def _stack8(rows, row8):
    out = jnp.broadcast_to(rows[0], row8.shape)
    for k in range(1, 8):
        out = jnp.where(row8 == k, rows[k], out)
    return out


def _peer_topk_kernel(h_ref, wqt_ref, sk_ref, e2_ref, th_ref, e1_ref, qt_ref, *, tile):
    qt_ref[...] = _nt(wqt_ref[...], h_ref[...]).astype(qt_ref.dtype)
    row8 = lax.broadcasted_iota(jnp.int32, (8, LANES), 0)

    def head(hd, carry):
        base = pl.multiple_of(hd * PEER_QUERY_DIM, PEER_QUERY_DIM)
        for lg in range(tile // LANES):
            ls = slice(lg * LANES, (lg + 1) * LANES)
            s1 = _nn(sk_ref[2 * hd], qt_ref[pl.ds(base, PEER_HALF), ls])
            s2 = _nn(sk_ref[2 * hd + 1], qt_ref[pl.ds(base + PEER_HALF, PEER_HALF), ls])
            t1 = _top16(s1)
            t2 = _top16(s2)
            t2a = _stack8(t2[:8], row8)
            t2b = _stack8(t2[8:], row8)
            t1b = _stack8(t1[8:], row8)
            cands = [t1[0] + t2a, t1[0] + t2b]
            for k in range(1, 8):
                cands.append(jnp.where(row8 < PEER_TOPK // (k + 1), t1[k] + t2a, -jnp.inf))
            cands.append(t1b + t2[0])
            c = jnp.concatenate(cands, axis=0)
            top = t1[0] + t2[0]
            z = jnp.zeros_like(top)
            tau = top
            for _ in range(PEER_TOPK):
                tau = jnp.max(c, axis=0, keepdims=True)
                z = z + jnp.exp(tau - top)
                c = jnp.where(c == tau, -jnp.inf, c)
            e2_of = lambda s: jnp.exp(s - t2[0]) * (0.5 / z)
            theta = jnp.full_like(s1, jnp.inf)
            for l in range(PEER_TOPK):
                theta = jnp.where(s1 + t2[l] >= tau, e2_of(t2[l]), theta)
            e2_ref[hd, lg] = e2_of(s2)
            th_ref[hd, lg] = theta
            e1_ref[hd, lg] = jnp.exp(s1 - t1[0])
        return carry

    lax.fori_loop(0, PEER_HEADS, head, 0)


def _peer_topk(h2, wqt, sk, tile):
    n = h2.shape[0]
    g = tile // LANES
    shp = jax.ShapeDtypeStruct((PEER_HEADS, n // LANES, PEER_N_KEYS, LANES), F32)
    spec = pl.BlockSpec((PEER_HEADS, g, PEER_N_KEYS, LANES), lambda i: (0, i, 0, 0))
    return pl.pallas_call(
        functools.partial(_peer_topk_kernel, tile=tile),
        out_shape=(shp, shp, shp),
        grid=(n // tile,),
        in_specs=[pl.BlockSpec((tile, D_MODEL), lambda i: (i, 0)), _full(wqt.shape), _full(sk.shape)],
        out_specs=(spec, spec, spec),
        scratch_shapes=[pltpu.VMEM((PEER_HEADS * PEER_QUERY_DIM, tile), MM_DTYPE)],
        compiler_params=_params("parallel"),
        name="peer_topk",
    )(h2, wqt, sk)


def _peer_main_kernel(h_ref, u_ref, vt_ref, e2_ref, th_ref, e1_ref, x1_ref, mod_ref, lng_ref, lnb_ref,
                      out_ref, at_ref, hm_ref, acc_ref, *, tile, eblk):
    j = pl.program_id(1)

    @pl.when(j == 0)
    def _():
        acc_ref[...] = jnp.zeros_like(acc_ref)

    at_ref[...] = _nt(u_ref[...], h_ref[...])
    for lg in range(tile // LANES):
        ls = slice(lg * LANES, (lg + 1) * LANES)
        for il in range(eblk // PEER_N_KEYS):
            rs = slice(il * PEER_N_KEYS, (il + 1) * PEER_N_KEYS)
            gate = None
            for hd in range(PEER_HEADS):
                e2 = e2_ref[hd, lg]
                w = jnp.where(e2 >= th_ref[hd, lg, il:il + 1, :], e1_ref[hd, lg, il:il + 1, :] * e2, 0.0)
                gate = w if gate is None else gate + w
            a = at_ref[rs, ls]
            hm_ref[rs, ls] = (gate * a * (1.0 + lax.erf(a * (2.0 ** -0.5)))).astype(hm_ref.dtype)
    acc_ref[...] += _nn(vt_ref[...], hm_ref[...])

    @pl.when(j == pl.num_programs(1) - 1)
    def _():
        f = acc_ref[...].T
        z = DEEPNORM_ALPHA * x1_ref[...] + mod_ref[0, 5:6, :] * f
        out_ref[...] = _layer_norm(z, lng_ref[...], lnb_ref[...])


def _peer_main(h2, u, vt, e2, th, e1, x1, mod, cond_fn, lng, lnb, tile, eblk):
    n = h2.shape[0]
    g = tile // LANES
    ik = eblk // PEER_N_KEYS
    row = lambda i, j: (i, 0)
    full_keys = pl.BlockSpec((PEER_HEADS, g, PEER_N_KEYS, LANES), lambda i, j: (0, i, 0, 0))
    slab_keys = pl.BlockSpec((PEER_HEADS, g, ik, LANES), lambda i, j: (0, i, j, 0))
    return pl.pallas_call(
        functools.partial(_peer_main_kernel, tile=tile, eblk=eblk),
        out_shape=jax.ShapeDtypeStruct((n, D_MODEL), F32),
        grid=(n // tile, PEER_N_EXPERTS // eblk),
        in_specs=[
            pl.BlockSpec((tile, D_MODEL), row),
            pl.BlockSpec((eblk, D_MODEL), lambda i, j: (j, 0)),
            pl.BlockSpec((D_MODEL, eblk), lambda i, j: (0, j)),
            full_keys, slab_keys, slab_keys,
            pl.BlockSpec((tile, D_MODEL), row),
            pl.BlockSpec((1, 6, D_MODEL), lambda i, j: (cond_fn(i), 0, 0)),
            pl.BlockSpec(lng.shape, lambda i, j: (0, 0)), pl.BlockSpec(lnb.shape, lambda i, j: (0, 0)),
        ],
        out_specs=pl.BlockSpec((tile, D_MODEL), row),
        scratch_shapes=[pltpu.VMEM((eblk, tile), F32), pltpu.VMEM((eblk, tile), MM_DTYPE),
                        pltpu.VMEM((D_MODEL, tile), F32)],
        compiler_params=_params("parallel", "arbitrary"),
        name="peer_main",
    )(h2, u, vt, e2, th, e1, x1, mod, lng, lnb)


def _peer_layer(h2, x1, mod, cond_fn, pw, lng, lnb, tile_topk, tile_main, eblk):
    e2, th, e1 = _peer_topk(h2, pw["wqt"], pw["sk"], tile_topk)
    return _peer_main(h2, pw["u"], pw["vt"], e2, th, e1, x1, mod, cond_fn, lng, lnb, tile_main, eblk)


def _pad_heads(w, n_used):
    k = w.shape[0]
    return jnp.pad(w, ((0, 0), (0, 0), (0, HEAD_PAD - n_used))).reshape(k, MLA_HEADS * HEAD_PAD)


def _rotate_half_cols(w):
    n = QK_ROPE_DIM // 4
    w4 = w.reshape(w.shape[:-1] + (2, 2, n))
    return jnp.stack([-w4[..., 1, :], w4[..., 0, :]], axis=-2).reshape(w.shape)


def _mla_weights(wdq, gq, wuq, wdkv, gkv, wukv):
    dq = QK_NOPE_DIM + QK_ROPE_DIM
    wuq3 = wuq.reshape(Q_LORA_RANK, MLA_HEADS, dq)
    pe = wuq3[..., QK_NOPE_DIM:]
    zeros_nope = jnp.zeros((Q_LORA_RANK, MLA_HEADS, QK_NOPE_DIM), F32)
    wuq_pad = _pad_heads(wuq3, dq)
    wuqs_pad = _pad_heads(jnp.concatenate([zeros_nope, _rotate_half_cols(pe)], axis=-1), dq)
    wkpe = wdkv[:, KV_LORA_RANK:]
    zk = jnp.zeros((D_MODEL, ROPE_OFF), F32)
    zt = jnp.zeros((D_MODEL, HEAD_PAD - ROPE_OFF - QK_ROPE_DIM), F32)
    wdkv_pad = jnp.concatenate([wdkv[:, :KV_LORA_RANK], zk, wkpe, zt, zk, _rotate_half_cols(wkpe), zt], axis=-1)
    wukv3 = wukv.reshape(KV_LORA_RANK, MLA_HEADS, QK_NOPE_DIM + V_HEAD_DIM)
    place = jnp.pad(jnp.eye(QK_ROPE_DIM, dtype=F32), ((0, 0), (ROPE_OFF, HEAD_PAD - ROPE_OFF - QK_ROPE_DIM)))
    return dict(
        wdq=wdq.astype(MM_DTYPE), gq=gq.reshape(1, -1), wuq=wuq_pad.astype(MM_DTYPE), wuqs=wuqs_pad.astype(MM_DTYPE),
        wdkv=wdkv_pad.astype(MM_DTYPE), gkv=gkv.reshape(1, -1),
        wukvk=_pad_heads(wukv3[..., :QK_NOPE_DIM], QK_NOPE_DIM).astype(MM_DTYPE),
        wukvv=wukv3[..., QK_NOPE_DIM:].reshape(KV_LORA_RANK, MLA_HEADS * V_HEAD_DIM).astype(MM_DTYPE),
        place=place.astype(MM_DTYPE),
    )


def _rope_tables(length):
    rows = length // GRID_W
    t = jnp.arange(rows * GRID_W)
    row = (t // GRID_W).astype(F32)
    col = (t % GRID_W).astype(F32)
    half = QK_ROPE_DIM // 2
    inv_freq = 1.0 / jnp.power(ROPE_THETA, jnp.arange(0, half, 2, dtype=F32) / half)
    n = QK_ROPE_DIM // 4
    ang = jnp.stack([row[:, None] * inv_freq, col[:, None] * inv_freq], axis=1)
    ang = jnp.broadcast_to(ang[:, :, None, :], (length, 2, 2, n)).reshape(length, QK_ROPE_DIM)
    pad = ((0, 0), (ROPE_OFF, HEAD_PAD - ROPE_OFF - QK_ROPE_DIM))
    cos = jnp.pad(jnp.cos(ang), pad, constant_values=1.0)
    sin = jnp.pad(jnp.sin(ang), pad)
    return cos, sin


def _dft_tables(length, scale):
    k = jnp.arange(length, dtype=jnp.int32)
    kt = (k[:, None] * k[None, :]) % length
    ang = kt.astype(F32) * (2.0 * math.pi / length)
    return (jnp.cos(ang) * scale).astype(MM_DTYPE), (jnp.sin(ang) * scale).astype(MM_DTYPE)


def kernel(x_prompt, x_sample, cache_ckv, cache_kpe, c, c_ctx, ada_w, ada_b, ln_mix_g, ln_mix_b, ln_ffn_g, ln_ffn_b, mla_wdq, mla_q_norm_g, mla_wuq, mla_wdkv, mla_kv_norm_g, mla_wukv, mla_wo, pool_w, pool_scale, fnet_w, fnet_b, peer_wq, peer_subkeys, peer_u, peer_v):
    nb_p, len_p, _ = x_prompt.shape
    nb_s, len_s, _ = x_sample.shape
    assert 1 + nb_s <= N_COND_ROWS
    cond = jnp.zeros((N_COND_ROWS, D_MODEL), F32).at[0].set(c_ctx).at[1:1 + nb_s].set(c)
    mods = _adaln(cond, ada_w, ada_b)

    streams = [
        dict(x=x_prompt.reshape(nb_p * len_p, D_MODEL), n_seq=nb_p, seq_len=len_p, per_seq=False),
        dict(x=x_sample.reshape(nb_s * len_s, D_MODEL), n_seq=nb_s, seq_len=len_s, per_seq=True),
    ]
    tile = 256
    tile_main = 512
    eblk = 1024
    cos_s, sin_s = _rope_tables(len_s)
    ident_cos = jnp.ones((tile, HEAD_PAD), F32)
    ident_sin = jnp.zeros((tile, HEAD_PAD), F32)
    new_ckv, new_kpe = [], []

    for i in range(DEPTH):
        kind, j = i % N_MIXERS, i // N_MIXERS
        mod = mods[i]
        lng, lnb = ln_mix_g[i].reshape(1, -1), ln_mix_b[i].reshape(1, -1)
        pw = dict(
            wqt=peer_wq[i].T.astype(MM_DTYPE),
            sk=peer_subkeys[i].reshape(PEER_HEADS * 2, PEER_N_KEYS, PEER_HALF).astype(MM_DTYPE),
            u=peer_u[i].astype(MM_DTYPE),
            vt=peer_v[i].T.astype(MM_DTYPE),
        )
        if kind == 0:
            mw = _mla_weights(mla_wdq[j], mla_q_norm_g[j], mla_wuq[j], mla_wdkv[j], mla_kv_norm_g[j], mla_wukv[j])
            wo = mla_wo[j].astype(MM_DTYPE)
            ctx = _mla_ctx(cache_ckv, cache_kpe, j, mw)
        elif kind == 2:
            cc, sc = _dft_tables(FOURIER_GROUP_DIM, FOURIER_GROUP_DIM ** -0.5)
        for s in streams:
            x, seq_len, n_seq = s["x"], s["seq_len"], s["n_seq"]
            cond_fn = _cond_fn(s["per_seq"], tile, seq_len)
            if kind == 0:
                if s["per_seq"]:
                    tps = seq_len // tile
                    q, k, v, _, _ = _mla_proj(x, mod, cond_fn, cos_s, sin_s, lambda t, tps=tps: t % tps, mw, tile)
                    o = _attention(q, k, v, n_seq, seq_len, min(256, seq_len), ctx=ctx)
                else:
                    q, k, v, ckv, kpe = _mla_proj(x, mod, cond_fn, ident_cos, ident_sin, lambda t: 0, mw, tile)
                    o = _attention(q, k, v, n_seq, seq_len, min(256, seq_len))
                    new_ckv.append(ckv.reshape(n_seq, seq_len, KV_LORA_RANK))
                    new_kpe.append(kpe.reshape(n_seq, seq_len, QK_ROPE_DIM))
                x1, h2 = _mla_post(o, x, mod, cond_fn, wo, lng, lnb, tile)
            elif kind == 1:
                x1, h2 = _pool_layer(x, mod, cond_fn, seq_len, pool_w[j].astype(MM_DTYPE),
                                     pool_scale[j].reshape(1, -1), lng, lnb, tile)
            else:
                hc, hs = _fnet_chan(x, mod, cond_fn, cc, sc, tile)
                cl, sl = _dft_tables(seq_len, seq_len ** -0.5)
                x1, h2 = _fnet_seq(hc, hs, x, mod, s["per_seq"], n_seq, seq_len, cl, sl,
                                   fnet_w[j].astype(MM_DTYPE), fnet_b[j].reshape(1, -1), lng, lnb,
                                   min(512, seq_len), min(1024, seq_len))
            cond_main = _cond_fn(s["per_seq"], min(tile_main, x.shape[0]), seq_len)
            s["x"] = _peer_layer(h2, x1, mod, cond_main, pw, ln_ffn_g[i].reshape(1, -1), ln_ffn_b[i].reshape(1, -1),
                                 tile, min(tile_main, x.shape[0]), eblk)

    y_prompt = streams[0]["x"].reshape(nb_p, len_p, D_MODEL)
    y_sample = streams[1]["x"].reshape(nb_s, len_s, D_MODEL)
    return (y_prompt, y_sample, jnp.stack(new_ckv, axis=1), jnp.stack(new_kpe, axis=1))
```

```python
import functools
import math

import jax
import jax.numpy as jnp
from jax import lax
from jax.experimental import pallas as pl
from jax.experimental.pallas import tpu as pltpu

F32 = jnp.float32
MM_DTYPE = jnp.bfloat16

D_MODEL = 1024
DEPTH = 4
GRID_W = 64
N_MIXERS = 3

MLA_HEADS = 16
QK_NOPE_DIM = 64
QK_ROPE_DIM = 32
V_HEAD_DIM = 64
Q_LORA_RANK = 384
KV_LORA_RANK = 256
ROPE_THETA = 10000.0
HEAD_PAD = 128
ROPE_OFF = QK_NOPE_DIM

POOL_WINDOWS = (2, 4, 8, 16)
N_POOL_GROUPS = 4
POOL_GROUP_DIM = D_MODEL // N_POOL_GROUPS
POOL_HALO = 8

N_FOURIER_GROUPS = 4
FOURIER_GROUP_DIM = D_MODEL // N_FOURIER_GROUPS

PEER_HEADS = 8
PEER_N_KEYS = 128
PEER_N_EXPERTS = PEER_N_KEYS * PEER_N_KEYS
PEER_QUERY_DIM = 256
PEER_HALF = PEER_QUERY_DIM // 2
PEER_TOPK = 16

DEEPNORM_ALPHA = (2 * DEPTH) ** 0.25
LN_EPS = 1e-5
RMS_EPS = 1e-6

LANES = 128
SUBLANES = 8
N_COND_ROWS = 16
VMEM_LIMIT = 56 * 1024 * 1024


def _nn(a, b):
    return jnp.dot(a, b, preferred_element_type=F32)


def _nt(a, b):
    return lax.dot_general(a, b, (((1,), (1,)), ((), ())), preferred_element_type=F32)


def _params(*sem, flags=None):
    return pltpu.CompilerParams(dimension_semantics=sem, vmem_limit_bytes=VMEM_LIMIT, flags=flags)


def _layer_norm(z, g, b):
    mu = jnp.mean(z, axis=-1, keepdims=True)
    d = z - mu
    var = jnp.mean(d * d, axis=-1, keepdims=True)
    return d * lax.rsqrt(var + LN_EPS) * g + b


def _rms_norm(x, g):
    return x * lax.rsqrt(jnp.mean(x * x, axis=-1, keepdims=True) + RMS_EPS) * g


def _full(shape):
    n = len(shape)
    return pl.BlockSpec(shape, lambda *_: (0,) * n)


def _mod_spec(cond_fn):
    return pl.BlockSpec((1, 6, D_MODEL), lambda i, *_: (cond_fn(i), 0, 0))


def _cond_fn(per_seq, tile, seq_len):
    if not per_seq:
        return lambda i: 0
    return lambda i: 1 + (i * tile) // seq_len


def _adaln_kernel(cond_ref, w_ref, b_ref, o_ref):
    c = cond_ref[...]
    c = c / (1.0 + jnp.exp(-c))
    o_ref[0] = jnp.dot(c, w_ref[0], preferred_element_type=F32,
                       precision=lax.Precision.HIGHEST) + b_ref[0]


def _adaln(cond, ada_w, ada_b):
    nb = 1536
    out = pl.pallas_call(
        _adaln_kernel,
        out_shape=jax.ShapeDtypeStruct((DEPTH, N_COND_ROWS, 6 * D_MODEL), F32),
        grid=(DEPTH, 6 * D_MODEL // nb),
        in_specs=[
            pl.BlockSpec((N_COND_ROWS, D_MODEL), lambda i, j: (0, 0)),
            pl.BlockSpec((1, D_MODEL, nb), lambda i, j: (i, 0, j)),
            pl.BlockSpec((1, 1, nb), lambda i, j: (i, 0, j)),
        ],
        out_specs=pl.BlockSpec((1, N_COND_ROWS, nb), lambda i, j: (i, 0, j)),
        compiler_params=_params("parallel", "parallel"),
        name="adaln",
    )(cond, ada_w, ada_b.reshape(DEPTH, 1, 6 * D_MODEL))
    return out.reshape(DEPTH, N_COND_ROWS, 6, D_MODEL)


def _mla_proj_kernel(x_ref, mod_ref, cos_ref, sin_ref, wdq_ref, gq_ref, wuq_ref, wuqs_ref,
                     wdkv_ref, gkv_ref, wukvk_ref, wukvv_ref,
                     q_ref, k_ref, v_ref, ckv_ref, kpe_ref):
    h = x_ref[...] * (1.0 + mod_ref[0, 1:2, :]) + mod_ref[0, 0:1, :]
    hb = h.astype(MM_DTYPE)
    cq = _rms_norm(_nn(hb, wdq_ref[...]), gq_ref[...]).astype(MM_DTYPE)
    q = _nn(cq, wuq_ref[...])
    qs = _nn(cq, wuqs_ref[...])
    kv = _nn(hb, wdkv_ref[...])
    ckv = _rms_norm(kv[:, :KV_LORA_RANK], gkv_ref[...])
    kpe = kv[:, KV_LORA_RANK:KV_LORA_RANK + HEAD_PAD]
    kpes = kv[:, KV_LORA_RANK + HEAD_PAD:]
    cos = cos_ref[...]
    sin = sin_ref[...]
    kpe_rot = kpe * cos + kpes * sin
    ckvb = ckv.astype(MM_DTYPE)
    kn = _nn(ckvb, wukvk_ref[...])
    scale = (QK_NOPE_DIM + QK_ROPE_DIM) ** -0.5
    for hd in range(MLA_HEADS):
        sl = slice(hd * HEAD_PAD, (hd + 1) * HEAD_PAD)
        q_ref[:, sl] = ((q[:, sl] * cos + qs[:, sl] * sin) * scale).astype(q_ref.dtype)
        k_ref[:, sl] = (kn[:, sl] + kpe_rot).astype(k_ref.dtype)
    v_ref[...] = _nn(ckvb, wukvv_ref[...]).astype(v_ref.dtype)
    ckv_ref[...] = ckv
    kpe_ref[...] = kpe[:, ROPE_OFF:ROPE_OFF + QK_ROPE_DIM]


def _mla_proj(x, mod, cond_fn, cos, sin, pos_fn, w, tile):
    n = x.shape[0]
    hp = MLA_HEADS * HEAD_PAD
    hv = MLA_HEADS * V_HEAD_DIM
    row = lambda i: (i, 0)
    return pl.pallas_call(
        _mla_proj_kernel,
        out_shape=(
            jax.ShapeDtypeStruct((n, hp), MM_DTYPE),
            jax.ShapeDtypeStruct((n, hp), MM_DTYPE),
            jax.ShapeDtypeStruct((n, hv), MM_DTYPE),
            jax.ShapeDtypeStruct((n, KV_LORA_RANK), F32),
            jax.ShapeDtypeStruct((n, QK_ROPE_DIM), F32),
        ),
        grid=(n // tile,),
        in_specs=[
            pl.BlockSpec((tile, D_MODEL), row),
            _mod_spec(cond_fn),
            pl.BlockSpec((tile, HEAD_PAD), lambda i: (pos_fn(i), 0)),
            pl.BlockSpec((tile, HEAD_PAD), lambda i: (pos_fn(i), 0)),
            _full(w["wdq"].shape), _full(w["gq"].shape), _full(w["wuq"].shape), _full(w["wuqs"].shape),
            _full(w["wdkv"].shape), _full(w["gkv"].shape), _full(w["wukvk"].shape), _full(w["wukvv"].shape),
        ],
        out_specs=(
            pl.BlockSpec((tile, hp), row),
            pl.BlockSpec((tile, hp), row),
            pl.BlockSpec((tile, hv), row),
            pl.BlockSpec((tile, KV_LORA_RANK), row),
            pl.BlockSpec((tile, QK_ROPE_DIM), row),
        ),
        compiler_params=_params("parallel"),
        name="mla_proj",
    )(x, mod, cos, sin, w["wdq"], w["gq"], w["wuq"], w["wuqs"], w["wdkv"], w["gkv"], w["wukvk"], w["wukvv"])


def _mla_ctx_kernel(ckv_ref, kpe_ref, wukvk_ref, wukvv_ref, place_ref, k_ref, v_ref):
    ckvb = ckv_ref[...].astype(MM_DTYPE)
    kn = _nn(ckvb, wukvk_ref[...])
    kpe = _nn(kpe_ref[...].astype(MM_DTYPE), place_ref[...])
    for hd in range(MLA_HEADS):
        sl = slice(hd * HEAD_PAD, (hd + 1) * HEAD_PAD)
        k_ref[:, sl] = (kn[:, sl] + kpe).astype(k_ref.dtype)
    v_ref[...] = _nn(ckvb, wukvv_ref[...]).astype(v_ref.dtype)


def _mla_ctx(cache_ckv, cache_kpe, j, w):
    nb, _, past, _ = cache_ckv.shape
    hp = MLA_HEADS * HEAD_PAD
    hv = MLA_HEADS * V_HEAD_DIM
    return pl.pallas_call(
        _mla_ctx_kernel,
        out_shape=(jax.ShapeDtypeStruct((nb * past, hp), MM_DTYPE),
                   jax.ShapeDtypeStruct((nb * past, hv), MM_DTYPE)),
        grid=(nb,),
        in_specs=[
            pl.BlockSpec((None, None, past, KV_LORA_RANK), lambda b: (b, j, 0, 0)),
            pl.BlockSpec((None, None, past, QK_ROPE_DIM), lambda b: (b, j, 0, 0)),
            _full(w["wukvk"].shape), _full(w["wukvv"].shape), _full(w["place"].shape),
        ],
        out_specs=(pl.BlockSpec((past, hp), lambda b: (b, 0)),
                   pl.BlockSpec((past, hv), lambda b: (b, 0))),
        compiler_params=_params("parallel"),
        name="mla_ctx",
    )(cache_ckv, cache_kpe, w["wukvk"], w["wukvv"], w["place"])


def _attn_kernel(*refs, has_ctx):
    if has_ctx:
        q_ref, k_ref, v_ref, kc_ref, vc_ref, o_ref = refs
    else:
        q_ref, k_ref, v_ref, o_ref = refs
    lane = lax.broadcasted_iota(jnp.int32, (1, 2 * V_HEAD_DIM), 1)
    out = None
    for hh in range(2):
        sl = slice(hh * HEAD_PAD, (hh + 1) * HEAD_PAD)
        own = (lane >= hh * V_HEAD_DIM) & (lane < (hh + 1) * V_HEAD_DIM)
        q = q_ref[:, sl]
        s = _nt(q, k_ref[:, sl])
        m = jnp.max(s, axis=-1, keepdims=True)
        if has_ctx:
            sc = _nt(q, kc_ref[:, sl])
            m = jnp.maximum(m, jnp.max(sc, axis=-1, keepdims=True))
        p = jnp.exp(s - m)
        l = jnp.sum(p, axis=-1, keepdims=True)
        v = v_ref[...]
        o = _nn(p.astype(MM_DTYPE), jnp.where(own, v, jnp.zeros_like(v)))
        if has_ctx:
            pc = jnp.exp(sc - m)
            l = l + jnp.sum(pc, axis=-1, keepdims=True)
            vc = vc_ref[...]
            o = o + _nn(pc.astype(MM_DTYPE), jnp.where(own, vc, jnp.zeros_like(vc)))
        o = o * (1.0 / l)
        out = o if out is None else out + o
    o_ref[...] = out.astype(o_ref.dtype)


def _attention(q, k, v, n_seq, seq_len, tq, ctx=None):
    n = q.shape[0]
    hpairs = MLA_HEADS // 2
    qb = seq_len // tq
    in_specs = [
        pl.BlockSpec((tq, 2 * HEAD_PAD), lambda b, h, i: (b * qb + i, h)),
        pl.BlockSpec((seq_len, 2 * HEAD_PAD), lambda b, h, i: (b, h)),
        pl.BlockSpec((seq_len, 2 * V_HEAD_DIM), lambda b, h, i: (b, h)),
    ]
    args = [q, k, v]
    if ctx is not None:
        kc, vc = ctx
        past = kc.shape[0] // n_seq
        in_specs += [
            pl.BlockSpec((past, 2 * HEAD_PAD), lambda b, h, i: (b, h)),
            pl.BlockSpec((past, 2 * V_HEAD_DIM), lambda b, h, i: (b, h)),
        ]
        args += [kc, vc]
    return pl.pallas_call(
        functools.partial(_attn_kernel, has_ctx=ctx is not None),
        out_shape=jax.ShapeDtypeStruct((n, MLA_HEADS * V_HEAD_DIM), MM_DTYPE),
        grid=(n_seq, hpairs, qb),
        in_specs=in_specs,
        out_specs=pl.BlockSpec((tq, 2 * V_HEAD_DIM), lambda b, h, i: (b * qb + i, h)),
        compiler_params=_params("parallel", "parallel", "arbitrary"),
        name="mla_attn",
    )(*args)


def _post_store(z, mod_ref, lng_ref, lnb_ref, x1_ref, h2_ref):
    x1 = _layer_norm(z, lng_ref[...], lnb_ref[...])
    x1_ref[...] = x1
    h2_ref[...] = (x1 * (1.0 + mod_ref[0, 4:5, :]) + mod_ref[0, 3:4, :]).astype(h2_ref.dtype)


def _mla_post_kernel(o_ref, x_ref, mod_ref, wo_ref, lng_ref, lnb_ref, x1_ref, h2_ref):
    y = _nn(o_ref[...], wo_ref[...])
    z = DEEPNORM_ALPHA * x_ref[...] + mod_ref[0, 2:3, :] * y
    _post_store(z, mod_ref, lng_ref, lnb_ref, x1_ref, h2_ref)


def _mla_post(o, x, mod, cond_fn, wo, lng, lnb, tile):
    n = x.shape[0]
    row = lambda i: (i, 0)
    return pl.pallas_call(
        _mla_post_kernel,
        out_shape=(jax.ShapeDtypeStruct((n, D_MODEL), F32), jax.ShapeDtypeStruct((n, D_MODEL), MM_DTYPE)),
        grid=(n // tile,),
        in_specs=[
            pl.BlockSpec((tile, D_MODEL), row), pl.BlockSpec((tile, D_MODEL), row), _mod_spec(cond_fn),
            _full(wo.shape), _full(lng.shape), _full(lnb.shape),
        ],
        out_specs=(pl.BlockSpec((tile, D_MODEL), row), pl.BlockSpec((tile, D_MODEL), row)),
        compiler_params=_params("parallel"),
        name="mla_post",
    )(o, x, mod, wo, lng, lnb)


def _pool_kernel(x_ref, xp_ref, xn_ref, mod_ref, w_ref, ps_ref, lng_ref, lnb_ref, x1_ref, h2_ref, ext_ref,
                 *, tile, tiles_per_seq):
    i = pl.program_id(0)
    pos = i % tiles_per_seq
    scale1 = 1.0 + mod_ref[0, 1:2, :]
    shift1 = mod_ref[0, 0:1, :]
    x = x_ref[...]
    keep_prev = (pos > 0).astype(F32)
    keep_next = (pos < tiles_per_seq - 1).astype(F32)
    ext_ref[0:POOL_HALO, :] = (xp_ref[...] * scale1 + shift1) * keep_prev
    ext_ref[POOL_HALO:POOL_HALO + tile, :] = x * scale1 + shift1
    ext_ref[POOL_HALO + tile:, :] = (xn_ref[...] * scale1 + shift1) * keep_next
    seq_len = tile * tiles_per_seq
    t = pos * tile + lax.broadcasted_iota(jnp.int32, (tile, POOL_GROUP_DIM), 0)
    ys = []
    for g, w in enumerate(POOL_WINDOWS):
        cs = slice(g * POOL_GROUP_DIM, (g + 1) * POOL_GROUP_DIM)
        tot = None
        for d in range(-(w // 2), w - w // 2):
            piece = ext_ref[POOL_HALO + d:POOL_HALO + d + tile, cs]
            tot = piece if tot is None else tot + piece
        lo = jnp.maximum(t - w // 2, 0)
        hi = jnp.minimum(t + (w - w // 2) - 1, seq_len - 1)
        cnt = (hi - lo + 1).astype(F32)
        diff = tot / cnt - ext_ref[POOL_HALO:POOL_HALO + tile, cs]
        ys.append(_nn(diff.astype(MM_DTYPE), w_ref[g]))
    y = jnp.concatenate(ys, axis=-1) * ps_ref[...]
    z = DEEPNORM_ALPHA * x + mod_ref[0, 2:3, :] * y
    _post_store(z, mod_ref, lng_ref, lnb_ref, x1_ref, h2_ref)


def _pool_layer(x, mod, cond_fn, seq_len, w, ps, lng, lnb, tile):
    n = x.shape[0]
    tps = seq_len // tile
    hb = tile // POOL_HALO
    nblk = n // POOL_HALO
    row = lambda i: (i, 0)
    return pl.pallas_call(
        functools.partial(_pool_kernel, tile=tile, tiles_per_seq=tps),
        out_shape=(jax.ShapeDtypeStruct((n, D_MODEL), F32), jax.ShapeDtypeStruct((n, D_MODEL), MM_DTYPE)),
        grid=(n // tile,),
        in_specs=[
            pl.BlockSpec((tile, D_MODEL), row),
            pl.BlockSpec((POOL_HALO, D_MODEL), lambda i: (jnp.maximum(i * hb - 1, 0), 0)),
            pl.BlockSpec((POOL_HALO, D_MODEL), lambda i: (jnp.minimum((i + 1) * hb, nblk - 1), 0)),
            _mod_spec(cond_fn), _full(w.shape), _full(ps.shape), _full(lng.shape), _full(lnb.shape),
        ],
        out_specs=(pl.BlockSpec((tile, D_MODEL), row), pl.BlockSpec((tile, D_MODEL), row)),
        scratch_shapes=[pltpu.VMEM((tile + 2 * POOL_HALO, D_MODEL), F32)],
        compiler_params=_params("parallel"),
        name="pool_layer",
    )(x, x, x, mod, w, ps, lng, lnb)


def _fnet_chan_kernel(x_ref, mod_ref, cc_ref, sc_ref, hc_ref, hs_ref):
    h = x_ref[...] * (1.0 + mod_ref[0, 1:2, :]) + mod_ref[0, 0:1, :]
    hb = h.astype(MM_DTYPE)
    for g in range(N_FOURIER_GROUPS):
        cs = slice(g * FOURIER_GROUP_DIM, (g + 1) * FOURIER_GROUP_DIM)
        hc_ref[:, cs] = _nn(hb[:, cs], cc_ref[...]).astype(hc_ref.dtype)
        hs_ref[:, cs] = _nn(hb[:, cs], sc_ref[...]).astype(hs_ref.dtype)


def _fnet_chan(x, mod, cond_fn, cc, sc, tile):
    n = x.shape[0]
    row = lambda i: (i, 0)
    return pl.pallas_call(
        _fnet_chan_kernel,
        out_shape=(jax.ShapeDtypeStruct((n, D_MODEL), MM_DTYPE), jax.ShapeDtypeStruct((n, D_MODEL), MM_DTYPE)),
        grid=(n // tile,),
        in_specs=[pl.BlockSpec((tile, D_MODEL), row), _mod_spec(cond_fn), _full(cc.shape), _full(sc.shape)],
        out_specs=(pl.BlockSpec((tile, D_MODEL), row), pl.BlockSpec((tile, D_MODEL), row)),
        compiler_params=_params("parallel"),
        name="fnet_chan",
    )(x, mod, cc, sc)


def _fnet_seq_kernel(cl_ref, sl_ref, hc_ref, hs_ref, x_ref, mod_ref, fw_ref, fb_ref, lng_ref, lnb_ref,
                     x1_ref, h2_ref, acc_ref):
    tt = pl.program_id(2)

    @pl.when(tt == 0)
    def _():
        acc_ref[...] = jnp.zeros_like(acc_ref)

    acc_ref[...] += _nn(cl_ref[...], hc_ref[...]) - _nn(sl_ref[...], hs_ref[...])

    @pl.when(tt == pl.num_programs(2) - 1)
    def _():
        y = _nn(acc_ref[...].astype(MM_DTYPE), fw_ref[...]) + fb_ref[...]
        z = DEEPNORM_ALPHA * x_ref[...] + mod_ref[0, 2:3, :] * y
        _post_store(z, mod_ref, lng_ref, lnb_ref, x1_ref, h2_ref)


def _fnet_seq(hc, hs, x, mod, per_seq, n_seq, seq_len, cl, sl, fw, fb, lng, lnb, tk, tt):
    n = x.shape[0]
    kb = seq_len // tk
    tb = seq_len // tt
    out_row = lambda b, k, t: (b * kb + k, 0)
    cond = (lambda b, k, t: (1 + b, 0, 0)) if per_seq else (lambda b, k, t: (0, 0, 0))
    return pl.pallas_call(
        _fnet_seq_kernel,
        out_shape=(jax.ShapeDtypeStruct((n, D_MODEL), F32), jax.ShapeDtypeStruct((n, D_MODEL), MM_DTYPE)),
        grid=(n_seq, kb, tb),
        in_specs=[
            pl.BlockSpec((tk, tt), lambda b, k, t: (k, t)),
            pl.BlockSpec((tk, tt), lambda b, k, t: (k, t)),
            pl.BlockSpec((tt, D_MODEL), lambda b, k, t: (b * tb + t, 0)),
            pl.BlockSpec((tt, D_MODEL), lambda b, k, t: (b * tb + t, 0)),
            pl.BlockSpec((tk, D_MODEL), out_row),
            pl.BlockSpec((1, 6, D_MODEL), cond),
            _full(fw.shape), _full(fb.shape), _full(lng.shape), _full(lnb.shape),
        ],
        out_specs=(pl.BlockSpec((tk, D_MODEL), out_row), pl.BlockSpec((tk, D_MODEL), out_row)),
        scratch_shapes=[pltpu.VMEM((tk, D_MODEL), F32)],
        compiler_params=_params("parallel", "parallel", "arbitrary"),
        name="fnet_seq",
    )(cl, sl, hc, hs, x, mod, fw, fb, lng, lnb)


def _top16(x):
    v = _sort_desc([x[SUBLANES * r:SUBLANES * (r + 1)] for r in range(x.shape[0] // SUBLANES)])
    rows = []
    for k in range(PEER_TOPK):
        m = jnp.max(v[0], axis=0, keepdims=True)
        rows.append(m)
        hit = v[0] == m
        v = [jnp.where(hit, v[d + 1], v[d]) for d in range(PEER_TOPK - 1 - k)]
    return rows


def _sort_desc(v):
    v = list(v)
    n = len(v)
    p = 1
    while p < n:
        k = p
        while k >= 1:
            for j in range(k % p, n - k, 2 * k):
                for i in range(min(k, n - j - k)):
                    if (i + j) // (2 * p) == (i + j + k) // (2 * p):
                        a, b = v[i + j], v[i + j + k]
                        v[i + j], v[i + j + k] = jnp.maximum(a, b), jnp.minimum(a, b)
            k //= 2
        p *= 2
    return v


def _stack8(rows, row8):
    out = jnp.broadcast_to(rows[0], row8.shape)
    for k in range(1, 8):
        out = jnp.where(row8 == k, rows[k], out)
    return out


def _peer_topk_kernel(h_ref, wqt_ref, sk_ref, e2_ref, th_ref, e1_ref, qt_ref, *, tile):
    qt_ref[...] = _nt(wqt_ref[...], h_ref[...]).astype(qt_ref.dtype)
    row8 = lax.broadcasted_iota(jnp.int32, (8, LANES), 0)

    def head(hd, carry):
        base = pl.multiple_of(hd * PEER_QUERY_DIM, PEER_QUERY_DIM)
        for lg in range(tile // LANES):
            ls = slice(lg * LANES, (lg + 1) * LANES)
            s1 = _nn(sk_ref[2 * hd], qt_ref[pl.ds(base, PEER_HALF), ls])
            s2 = _nn(sk_ref[2 * hd + 1], qt_ref[pl.ds(base + PEER_HALF, PEER_HALF), ls])
            t1 = _top16(s1)
            t2 = _top16(s2)
            t2a = _stack8(t2[:8], row8)
            t2b = _stack8(t2[8:], row8)
            t1b = _stack8(t1[8:], row8)
            cands = [t1[0] + t2a, t1[0] + t2b]
            for k in range(1, 8):
                cands.append(jnp.where(row8 < PEER_TOPK // (k + 1), t1[k] + t2a, -jnp.inf))
            cands.append(t1b + t2[0])
            c = jnp.concatenate(cands, axis=0)
            top = t1[0] + t2[0]
            z = jnp.zeros_like(top)
            tau = top
            for _ in range(PEER_TOPK):
                tau = jnp.max(c, axis=0, keepdims=True)
                z = z + jnp.exp(tau - top)
                c = jnp.where(c == tau, -jnp.inf, c)
            e2_of = lambda s: jnp.exp(s - t2[0]) * (0.5 / z)
            theta = jnp.full_like(s1, jnp.inf)
            for l in range(PEER_TOPK):
                theta = jnp.where(s1 + t2[l] >= tau, e2_of(t2[l]), theta)
            e2_ref[hd, lg] = e2_of(s2)
            th_ref[hd, lg] = theta
            e1_ref[hd, lg] = jnp.exp(s1 - t1[0])
        return carry

    lax.fori_loop(0, PEER_HEADS, head, 0)


def _peer_topk(h2, wqt, sk, tile):
    n = h2.shape[0]
    g = tile // LANES
    shp = jax.ShapeDtypeStruct((PEER_HEADS, n // LANES, PEER_N_KEYS, LANES), F32)
    spec = pl.BlockSpec((PEER_HEADS, g, PEER_N_KEYS, LANES), lambda i: (0, i, 0, 0))
    return pl.pallas_call(
        functools.partial(_peer_topk_kernel, tile=tile),
        out_shape=(shp, shp, shp),
        grid=(n // tile,),
        in_specs=[pl.BlockSpec((tile, D_MODEL), lambda i: (i, 0)), _full(wqt.shape), _full(sk.shape)],
        out_specs=(spec, spec, spec),
        scratch_shapes=[pltpu.VMEM((PEER_HEADS * PEER_QUERY_DIM, tile), MM_DTYPE)],
        compiler_params=_params("parallel"),
        name="peer_topk",
    )(h2, wqt, sk)


def _peer_main_kernel(h_ref, u_ref, vt_ref, e2_ref, th_ref, e1_ref, x1_ref, mod_ref, lng_ref, lnb_ref,
                      out_ref, at_ref, hm_ref, acc_ref, *, tile, eblk):
    j = pl.program_id(1)

    @pl.when(j == 0)
    def _():
        acc_ref[...] = jnp.zeros_like(acc_ref)

    at_ref[...] = _nt(u_ref[...], h_ref[...])
    for lg in range(tile // LANES):
        ls = slice(lg * LANES, (lg + 1) * LANES)
        for il in range(eblk // PEER_N_KEYS):
            rs = slice(il * PEER_N_KEYS, (il + 1) * PEER_N_KEYS)
            gate = None
            for hd in range(PEER_HEADS):
                e2 = e2_ref[hd, lg]
                w = jnp.where(e2 >= th_ref[hd, lg, il:il + 1, :], e1_ref[hd, lg, il:il + 1, :] * e2, 0.0)
                gate = w if gate is None else gate + w
            a = at_ref[rs, ls]
            hm_ref[rs, ls] = (gate * a * (1.0 + lax.erf(a * (2.0 ** -0.5)))).astype(hm_ref.dtype)
    acc_ref[...] += _nn(vt_ref[...], hm_ref[...])

    @pl.when(j == pl.num_programs(1) - 1)
    def _():
        f = acc_ref[...].T
        z = DEEPNORM_ALPHA * x1_ref[...] + mod_ref[0, 5:6, :] * f
        out_ref[...] = _layer_norm(z, lng_ref[...], lnb_ref[...])


def _peer_main(h2, u, vt, e2, th, e1, x1, mod, cond_fn, lng, lnb, tile, eblk):
    n = h2.shape[0]
    g = tile // LANES
    ik = eblk // PEER_N_KEYS
    row = lambda i, j: (i, 0)
    full_keys = pl.BlockSpec((PEER_HEADS, g, PEER_N_KEYS, LANES), lambda i, j: (0, i, 0, 0))
    slab_keys = pl.BlockSpec((PEER_HEADS, g, ik, LANES), lambda i, j: (0, i, j, 0))
    return pl.pallas_call(
        functools.partial(_peer_main_kernel, tile=tile, eblk=eblk),
        out_shape=jax.ShapeDtypeStruct((n, D_MODEL), F32),
        grid=(n // tile, PEER_N_EXPERTS // eblk),
        in_specs=[
            pl.BlockSpec((tile, D_MODEL), row),
            pl.BlockSpec((eblk, D_MODEL), lambda i, j: (j, 0)),
            pl.BlockSpec((D_MODEL, eblk), lambda i, j: (0, j)),
            full_keys, slab_keys, slab_keys,
            pl.BlockSpec((tile, D_MODEL), row),
            pl.BlockSpec((1, 6, D_MODEL), lambda i, j: (cond_fn(i), 0, 0)),
            pl.BlockSpec(lng.shape, lambda i, j: (0, 0)), pl.BlockSpec(lnb.shape, lambda i, j: (0, 0)),
        ],
        out_specs=pl.BlockSpec((tile, D_MODEL), row),
        scratch_shapes=[pltpu.VMEM((eblk, tile), F32), pltpu.VMEM((eblk, tile), MM_DTYPE),
                        pltpu.VMEM((D_MODEL, tile), F32)],
        compiler_params=_params("parallel", "arbitrary"),
        name="peer_main",
    )(h2, u, vt, e2, th, e1, x1, mod, lng, lnb)


def _peer_layer(h2, x1, mod, cond_fn, pw, lng, lnb, tile_topk, tile_main, eblk):
    e2, th, e1 = _peer_topk(h2, pw["wqt"], pw["sk"], tile_topk)
    return _peer_main(h2, pw["u"], pw["vt"], e2, th, e1, x1, mod, cond_fn, lng, lnb, tile_main, eblk)


def _pad_heads(w, n_used):
    k = w.shape[0]
    return jnp.pad(w, ((0, 0), (0, 0), (0, HEAD_PAD - n_used))).reshape(k, MLA_HEADS * HEAD_PAD)


def _rotate_half_cols(w):
    n = QK_ROPE_DIM // 4
    w4 = w.reshape(w.shape[:-1] + (2, 2, n))
    return jnp.stack([-w4[..., 1, :], w4[..., 0, :]], axis=-2).reshape(w.shape)


def _mla_weights(wdq, gq, wuq, wdkv, gkv, wukv):
    dq = QK_NOPE_DIM + QK_ROPE_DIM
    wuq3 = wuq.reshape(Q_LORA_RANK, MLA_HEADS, dq)
    pe = wuq3[..., QK_NOPE_DIM:]
    zeros_nope = jnp.zeros((Q_LORA_RANK, MLA_HEADS, QK_NOPE_DIM), F32)
    wuq_pad = _pad_heads(wuq3, dq)
    wuqs_pad = _pad_heads(jnp.concatenate([zeros_nope, _rotate_half_cols(pe)], axis=-1), dq)
    wkpe = wdkv[:, KV_LORA_RANK:]
    zk = jnp.zeros((D_MODEL, ROPE_OFF), F32)
    zt = jnp.zeros((D_MODEL, HEAD_PAD - ROPE_OFF - QK_ROPE_DIM), F32)
    wdkv_pad = jnp.concatenate([wdkv[:, :KV_LORA_RANK], zk, wkpe, zt, zk, _rotate_half_cols(wkpe), zt], axis=-1)
    wukv3 = wukv.reshape(KV_LORA_RANK, MLA_HEADS, QK_NOPE_DIM + V_HEAD_DIM)
    place = jnp.pad(jnp.eye(QK_ROPE_DIM, dtype=F32), ((0, 0), (ROPE_OFF, HEAD_PAD - ROPE_OFF - QK_ROPE_DIM)))
    return dict(
        wdq=wdq.astype(MM_DTYPE), gq=gq.reshape(1, -1), wuq=wuq_pad.astype(MM_DTYPE), wuqs=wuqs_pad.astype(MM_DTYPE),
        wdkv=wdkv_pad.astype(MM_DTYPE), gkv=gkv.reshape(1, -1),
        wukvk=_pad_heads(wukv3[..., :QK_NOPE_DIM], QK_NOPE_DIM).astype(MM_DTYPE),
        wukvv=wukv3[..., QK_NOPE_DIM:].reshape(KV_LORA_RANK, MLA_HEADS * V_HEAD_DIM).astype(MM_DTYPE),
        place=place.astype(MM_DTYPE),
    )


def _rope_tables(length):
    rows = length // GRID_W
    t = jnp.arange(rows * GRID_W)
    row = (t // GRID_W).astype(F32)
    col = (t % GRID_W).astype(F32)
    half = QK_ROPE_DIM // 2
    inv_freq = 1.0 / jnp.power(ROPE_THETA, jnp.arange(0, half, 2, dtype=F32) / half)
    n = QK_ROPE_DIM // 4
    ang = jnp.stack([row[:, None] * inv_freq, col[:, None] * inv_freq], axis=1)
    ang = jnp.broadcast_to(ang[:, :, None, :], (length, 2, 2, n)).reshape(length, QK_ROPE_DIM)
    pad = ((0, 0), (ROPE_OFF, HEAD_PAD - ROPE_OFF - QK_ROPE_DIM))
    cos = jnp.pad(jnp.cos(ang), pad, constant_values=1.0)
    sin = jnp.pad(jnp.sin(ang), pad)
    return cos, sin


def _dft_tables(length, scale):
    k = jnp.arange(length, dtype=jnp.int32)
    kt = (k[:, None] * k[None, :]) % length
    ang = kt.astype(F32) * (2.0 * math.pi / length)
    return (jnp.cos(ang) * scale).astype(MM_DTYPE), (jnp.sin(ang) * scale).astype(MM_DTYPE)


def kernel(x_prompt, x_sample, cache_ckv, cache_kpe, c, c_ctx, ada_w, ada_b, ln_mix_g, ln_mix_b, ln_ffn_g, ln_ffn_b, mla_wdq, mla_q_norm_g, mla_wuq, mla_wdkv, mla_kv_norm_g, mla_wukv, mla_wo, pool_w, pool_scale, fnet_w, fnet_b, peer_wq, peer_subkeys, peer_u, peer_v):
    nb_p, len_p, _ = x_prompt.shape
    nb_s, len_s, _ = x_sample.shape
    assert 1 + nb_s <= N_COND_ROWS
    cond = jnp.zeros((N_COND_ROWS, D_MODEL), F32).at[0].set(c_ctx).at[1:1 + nb_s].set(c)
    mods = _adaln(cond, ada_w, ada_b)

    streams = [
        dict(x=x_prompt.reshape(nb_p * len_p, D_MODEL), n_seq=nb_p, seq_len=len_p, per_seq=False),
        dict(x=x_sample.reshape(nb_s * len_s, D_MODEL), n_seq=nb_s, seq_len=len_s, per_seq=True),
    ]
    tile = 256
    tile_main = 1024
    eblk = 1024
    cos_s, sin_s = _rope_tables(len_s)
    ident_cos = jnp.ones((tile, HEAD_PAD), F32)
    ident_sin = jnp.zeros((tile, HEAD_PAD), F32)
    new_ckv, new_kpe = [], []

    for i in range(DEPTH):
        kind, j = i % N_MIXERS, i // N_MIXERS
        mod = mods[i]
        lng, lnb = ln_mix_g[i].reshape(1, -1), ln_mix_b[i].reshape(1, -1)
        pw = dict(
            wqt=peer_wq[i].T.astype(MM_DTYPE),
            sk=peer_subkeys[i].reshape(PEER_HEADS * 2, PEER_N_KEYS, PEER_HALF).astype(MM_DTYPE),
            u=peer_u[i].astype(MM_DTYPE),
            vt=peer_v[i].T.astype(MM_DTYPE),
        )
        if kind == 0:
            mw = _mla_weights(mla_wdq[j], mla_q_norm_g[j], mla_wuq[j], mla_wdkv[j], mla_kv_norm_g[j], mla_wukv[j])
            wo = mla_wo[j].astype(MM_DTYPE)
            ctx = _mla_ctx(cache_ckv, cache_kpe, j, mw)
        elif kind == 2:
            cc, sc = _dft_tables(FOURIER_GROUP_DIM, FOURIER_GROUP_DIM ** -0.5)
        for s in streams:
            x, seq_len, n_seq = s["x"], s["seq_len"], s["n_seq"]
            cond_fn = _cond_fn(s["per_seq"], tile, seq_len)
            if kind == 0:
                if s["per_seq"]:
                    tps = seq_len // tile
                    q, k, v, _, _ = _mla_proj(x, mod, cond_fn, cos_s, sin_s, lambda t, tps=tps: t % tps, mw, tile)
                    o = _attention(q, k, v, n_seq, seq_len, min(256, seq_len), ctx=ctx)
                else:
                    q, k, v, ckv, kpe = _mla_proj(x, mod, cond_fn, ident_cos, ident_sin, lambda t: 0, mw, tile)
                    o = _attention(q, k, v, n_seq, seq_len, min(256, seq_len))
                    new_ckv.append(ckv.reshape(n_seq, seq_len, KV_LORA_RANK))
                    new_kpe.append(kpe.reshape(n_seq, seq_len, QK_ROPE_DIM))
                x1, h2 = _mla_post(o, x, mod, cond_fn, wo, lng, lnb, tile)
            elif kind == 1:
                x1, h2 = _pool_layer(x, mod, cond_fn, seq_len, pool_w[j].astype(MM_DTYPE),
                                     pool_scale[j].reshape(1, -1), lng, lnb, tile)
            else:
                hc, hs = _fnet_chan(x, mod, cond_fn, cc, sc, tile)
                cl, sl = _dft_tables(seq_len, seq_len ** -0.5)
                x1, h2 = _fnet_seq(hc, hs, x, mod, s["per_seq"], n_seq, seq_len, cl, sl,
                                   fnet_w[j].astype(MM_DTYPE), fnet_b[j].reshape(1, -1), lng, lnb,
                                   min(512, seq_len), min(1024, seq_len))
            cond_main = _cond_fn(s["per_seq"], min(tile_main, x.shape[0]), seq_len)
            s["x"] = _peer_layer(h2, x1, mod, cond_main, pw, ln_ffn_g[i].reshape(1, -1), ln_ffn_b[i].reshape(1, -1),
                                 tile, min(tile_main, x.shape[0]), eblk)

    y_prompt = streams[0]["x"].reshape(nb_p, len_p, D_MODEL)
    y_sample = streams[1]["x"].reshape(nb_s, len_s, D_MODEL)
    return (y_prompt, y_sample, jnp.stack(new_ckv, axis=1), jnp.stack(new_kpe, axis=1))
```

```python
import functools
import math

import jax
import jax.numpy as jnp
from jax import lax
from jax.experimental import pallas as pl
from jax.experimental.pallas import tpu as pltpu

F32 = jnp.float32
MM_DTYPE = jnp.bfloat16

D_MODEL = 1024
DEPTH = 4
GRID_W = 64
N_MIXERS = 3

MLA_HEADS = 16
QK_NOPE_DIM = 64
QK_ROPE_DIM = 32
V_HEAD_DIM = 64
Q_LORA_RANK = 384
KV_LORA_RANK = 256
ROPE_THETA = 10000.0
HEAD_PAD = 128
ROPE_OFF = QK_NOPE_DIM

POOL_WINDOWS = (2, 4, 8, 16)
N_POOL_GROUPS = 4
POOL_GROUP_DIM = D_MODEL // N_POOL_GROUPS
POOL_HALO = 8

N_FOURIER_GROUPS = 4
FOURIER_GROUP_DIM = D_MODEL // N_FOURIER_GROUPS

PEER_HEADS = 8
PEER_N_KEYS = 128
PEER_N_EXPERTS = PEER_N_KEYS * PEER_N_KEYS
PEER_QUERY_DIM = 256
PEER_HALF = PEER_QUERY_DIM // 2
PEER_TOPK = 16

DEEPNORM_ALPHA = (2 * DEPTH) ** 0.25
LN_EPS = 1e-5
RMS_EPS = 1e-6

LANES = 128
SUBLANES = 8
KEY_CHUNK = 512
N_COND_ROWS = 16
VMEM_LIMIT = 56 * 1024 * 1024


def _nn(a, b):
    return jnp.dot(a, b, preferred_element_type=F32)


def _nt(a, b):
    return lax.dot_general(a, b, (((1,), (1,)), ((), ())), preferred_element_type=F32)


def _params(*sem, flags=None):
    return pltpu.CompilerParams(dimension_semantics=sem, vmem_limit_bytes=VMEM_LIMIT, flags=flags)


def _layer_norm(z, g, b):
    mu = jnp.mean(z, axis=-1, keepdims=True)
    d = z - mu
    var = jnp.mean(d * d, axis=-1, keepdims=True)
    return d * lax.rsqrt(var + LN_EPS) * g + b


def _rms_norm(x, g):
    return x * lax.rsqrt(jnp.mean(x * x, axis=-1, keepdims=True) + RMS_EPS) * g


def _full(shape):
    n = len(shape)
    return pl.BlockSpec(shape, lambda *_: (0,) * n)


def _mod_spec(cond_fn):
    return pl.BlockSpec((1, 6, D_MODEL), lambda i, *_: (cond_fn(i), 0, 0))


def _cond_fn(per_seq, tile, seq_len):
    if not per_seq:
        return lambda i: 0
    return lambda i: 1 + (i * tile) // seq_len


def _adaln_kernel(cond_ref, w_ref, b_ref, o_ref):
    c = cond_ref[...]
    c = c / (1.0 + jnp.exp(-c))
    o_ref[0] = jnp.dot(c, w_ref[0], preferred_element_type=F32,
                       precision=lax.Precision.HIGHEST) + b_ref[0]


def _adaln(cond, ada_w, ada_b):
    nb = 1536
    out = pl.pallas_call(
        _adaln_kernel,
        out_shape=jax.ShapeDtypeStruct((DEPTH, N_COND_ROWS, 6 * D_MODEL), F32),
        grid=(DEPTH, 6 * D_MODEL // nb),
        in_specs=[
            pl.BlockSpec((N_COND_ROWS, D_MODEL), lambda i, j: (0, 0)),
            pl.BlockSpec((1, D_MODEL, nb), lambda i, j: (i, 0, j)),
            pl.BlockSpec((1, 1, nb), lambda i, j: (i, 0, j)),
        ],
        out_specs=pl.BlockSpec((1, N_COND_ROWS, nb), lambda i, j: (i, 0, j)),
        compiler_params=_params("parallel", "parallel"),
        name="adaln",
    )(cond, ada_w, ada_b.reshape(DEPTH, 1, 6 * D_MODEL))
    return out.reshape(DEPTH, N_COND_ROWS, 6, D_MODEL)


def _mla_proj_kernel(x_ref, mod_ref, cos_ref, sin_ref, wdq_ref, gq_ref, wuq_ref, wuqs_ref,
                     wdkv_ref, gkv_ref, wukvk_ref, wukvv_ref,
                     q_ref, k_ref, v_ref, ckv_ref, kpe_ref):
    h = x_ref[...] * (1.0 + mod_ref[0, 1:2, :]) + mod_ref[0, 0:1, :]
    hb = h.astype(MM_DTYPE)
    cq = _rms_norm(_nn(hb, wdq_ref[...]), gq_ref[...]).astype(MM_DTYPE)
    q = _nn(cq, wuq_ref[...])
    qs = _nn(cq, wuqs_ref[...])
    kv = _nn(hb, wdkv_ref[...])
    ckv = _rms_norm(kv[:, :KV_LORA_RANK], gkv_ref[...])
    kpe = kv[:, KV_LORA_RANK:KV_LORA_RANK + HEAD_PAD]
    kpes = kv[:, KV_LORA_RANK + HEAD_PAD:]
    cos = cos_ref[...]
    sin = sin_ref[...]
    kpe_rot = kpe * cos + kpes * sin
    ckvb = ckv.astype(MM_DTYPE)
    kn = _nn(ckvb, wukvk_ref[...])
    scale = (QK_NOPE_DIM + QK_ROPE_DIM) ** -0.5 * math.log2(math.e)
    for hd in range(MLA_HEADS):
        sl = slice(hd * HEAD_PAD, (hd + 1) * HEAD_PAD)
        q_ref[:, sl] = ((q[:, sl] * cos + qs[:, sl] * sin) * scale).astype(q_ref.dtype)
        k_ref[:, sl] = (kn[:, sl] + kpe_rot).astype(k_ref.dtype)
    v_ref[0], v_ref[1] = _split_values(_nn(ckvb, wukvv_ref[...]), v_ref.dtype)
    ckv_ref[...] = ckv
    kpe_ref[...] = kpe[:, ROPE_OFF:ROPE_OFF + QK_ROPE_DIM]


def _mla_proj(x, mod, cond_fn, cos, sin, pos_fn, w, tile):
    n = x.shape[0]
    hp = MLA_HEADS * HEAD_PAD
    hv = MLA_HEADS * V_HEAD_DIM
    row = lambda i: (i, 0)
    return pl.pallas_call(
        _mla_proj_kernel,
        out_shape=(
            jax.ShapeDtypeStruct((n, hp), MM_DTYPE),
            jax.ShapeDtypeStruct((n, hp), MM_DTYPE),
            jax.ShapeDtypeStruct((2, n, hv), MM_DTYPE),
            jax.ShapeDtypeStruct((n, KV_LORA_RANK), F32),
            jax.ShapeDtypeStruct((n, QK_ROPE_DIM), F32),
        ),
        grid=(n // tile,),
        in_specs=[
            pl.BlockSpec((tile, D_MODEL), row),
            _mod_spec(cond_fn),
            pl.BlockSpec((tile, HEAD_PAD), lambda i: (pos_fn(i), 0)),
            pl.BlockSpec((tile, HEAD_PAD), lambda i: (pos_fn(i), 0)),
            _full(w["wdq"].shape), _full(w["gq"].shape), _full(w["wuq"].shape), _full(w["wuqs"].shape),
            _full(w["wdkv"].shape), _full(w["gkv"].shape), _full(w["wukvk"].shape), _full(w["wukvv"].shape),
        ],
        out_specs=(
            pl.BlockSpec((tile, hp), row),
            pl.BlockSpec((tile, hp), row),
            pl.BlockSpec((2, tile, hv), lambda i: (0, i, 0)),
            pl.BlockSpec((tile, KV_LORA_RANK), row),
            pl.BlockSpec((tile, QK_ROPE_DIM), row),
        ),
        compiler_params=_params("parallel"),
        name="mla_proj",
    )(x, mod, cos, sin, w["wdq"], w["gq"], w["wuq"], w["wuqs"], w["wdkv"], w["gkv"], w["wukvk"], w["wukvv"])


def _mla_ctx_kernel(ckv_ref, kpe_ref, wukvk_ref, wukvv_ref, place_ref, k_ref, v_ref):
    ckvb = ckv_ref[...].astype(MM_DTYPE)
    kn = _nn(ckvb, wukvk_ref[...])
    kpe = _nn(kpe_ref[...].astype(MM_DTYPE), place_ref[...])
    for hd in range(MLA_HEADS):
        sl = slice(hd * HEAD_PAD, (hd + 1) * HEAD_PAD)
        k_ref[:, sl] = (kn[:, sl] + kpe).astype(k_ref.dtype)
    v_ref[0], v_ref[1] = _split_values(_nn(ckvb, wukvv_ref[...]), v_ref.dtype)


def _mla_ctx(cache_ckv, cache_kpe, j, w):
    nb, _, past, _ = cache_ckv.shape
    hp = MLA_HEADS * HEAD_PAD
    hv = MLA_HEADS * V_HEAD_DIM
    return pl.pallas_call(
        _mla_ctx_kernel,
        out_shape=(jax.ShapeDtypeStruct((nb * past, hp), MM_DTYPE),
                   jax.ShapeDtypeStruct((2, nb * past, hv), MM_DTYPE)),
        grid=(nb,),
        in_specs=[
            pl.BlockSpec((None, None, past, KV_LORA_RANK), lambda b: (b, j, 0, 0)),
            pl.BlockSpec((None, None, past, QK_ROPE_DIM), lambda b: (b, j, 0, 0)),
            _full(w["wukvk"].shape), _full(w["wukvv"].shape), _full(w["place"].shape),
        ],
        out_specs=(pl.BlockSpec((past, hp), lambda b: (b, 0)),
                   pl.BlockSpec((2, past, hv), lambda b: (0, b, 0))),
        compiler_params=_params("parallel"),
        name="mla_ctx",
    )(cache_ckv, cache_kpe, w["wukvk"], w["wukvv"], w["place"])


def _attn_kernel(*refs, has_ctx):
    if has_ctx:
        q_ref, k_ref, v_ref, kc_ref, vc_ref, o_ref = refs
    else:
        q_ref, k_ref, v_ref, o_ref = refs
    scores = []
    for hh in range(2):
        sl = slice(hh * HEAD_PAD, (hh + 1) * HEAD_PAD)
        q = q_ref[:, sl]
        s = _nt(q, k_ref[:, sl])
        m = jnp.max(s, axis=-1, keepdims=True)
        sc = None
        if has_ctx:
            sc = _nt(q, kc_ref[:, sl])
            m = jnp.maximum(m, jnp.max(sc, axis=-1, keepdims=True))
        scores.append((s, sc, m))
    outs = []
    for hh, (s, sc, m) in enumerate(scores):
        pieces = [(s, v_ref, c) for c in range(0, s.shape[1], KEY_CHUNK)]
        if has_ctx:
            pieces += [(sc, vc_ref, c) for c in range(0, sc.shape[1], KEY_CHUNK)]
        o = None
        for arr, vals, c in pieces:
            n = min(KEY_CHUNK, arr.shape[1] - c)
            d = _nn(jnp.exp2(arr[:, c:c + n] - m).astype(MM_DTYPE), vals[hh, c:c + n, :])
            o = d if o is None else o + d
        ol = _ones_lane(hh)
        outs.append(o * (1.0 / o[:, ol:ol + 1]))
    lane = lax.broadcasted_iota(jnp.int32, (1, 2 * V_HEAD_DIM), 1)
    o_ref[...] = jnp.where(lane < V_HEAD_DIM, outs[0], outs[1]).astype(o_ref.dtype)


def _ones_lane(hh):
    return V_HEAD_DIM if hh == 0 else 0


def _split_values(v, dtype):
    lane = lax.broadcasted_iota(jnp.int32, (1, 2 * V_HEAD_DIM), 1)
    parts = []
    for hh in range(2):
        own = (lane < V_HEAD_DIM) if hh == 0 else (lane >= V_HEAD_DIM)
        fill = jnp.where(lane == _ones_lane(hh), 1.0, 0.0)
        blocks = [jnp.where(own, v[:, p * 2 * V_HEAD_DIM:(p + 1) * 2 * V_HEAD_DIM], fill)
                  for p in range(MLA_HEADS // 2)]
        parts.append(jnp.concatenate(blocks, axis=-1).astype(dtype))
    return parts


def _attention(q, k, v, n_seq, seq_len, tq, ctx=None):
    n = q.shape[0]
    hpairs = MLA_HEADS // 2
    qb = seq_len // tq
    in_specs = [
        pl.BlockSpec((tq, 2 * HEAD_PAD), lambda b, h, i: (b * qb + i, h)),
        pl.BlockSpec((seq_len, 2 * HEAD_PAD), lambda b, h, i: (b, h)),
        pl.BlockSpec((2, seq_len, 2 * V_HEAD_DIM), lambda b, h, i: (0, b, h)),
    ]
    args = [q, k, v]
    if ctx is not None:
        kc, vc = ctx
        past = kc.shape[0] // n_seq
        in_specs += [
            pl.BlockSpec((past, 2 * HEAD_PAD), lambda b, h, i: (b, h)),
            pl.BlockSpec((2, past, 2 * V_HEAD_DIM), lambda b, h, i: (0, b, h)),
        ]
        args += [kc, vc]
    return pl.pallas_call(
        functools.partial(_attn_kernel, has_ctx=ctx is not None),
        out_shape=jax.ShapeDtypeStruct((n, MLA_HEADS * V_HEAD_DIM), MM_DTYPE),
        grid=(n_seq, hpairs, qb),
        in_specs=in_specs,
        out_specs=pl.BlockSpec((tq, 2 * V_HEAD_DIM), lambda b, h, i: (b * qb + i, h)),
        compiler_params=_params("parallel", "parallel", "arbitrary"),
        name="mla_attn",
    )(*args)


def _post_store(z, mod_ref, lng_ref, lnb_ref, x1_ref, h2_ref):
    x1 = _layer_norm(z, lng_ref[...], lnb_ref[...])
    x1_ref[...] = x1
    h2_ref[...] = (x1 * (1.0 + mod_ref[0, 4:5, :]) + mod_ref[0, 3:4, :]).astype(h2_ref.dtype)


def _mla_post_kernel(o_ref, x_ref, mod_ref, wo_ref, lng_ref, lnb_ref, x1_ref, h2_ref):
    y = _nn(o_ref[...], wo_ref[...])
    z = DEEPNORM_ALPHA * x_ref[...] + mod_ref[0, 2:3, :] * y
    _post_store(z, mod_ref, lng_ref, lnb_ref, x1_ref, h2_ref)


def _mla_post(o, x, mod, cond_fn, wo, lng, lnb, tile):
    n = x.shape[0]
    row = lambda i: (i, 0)
    return pl.pallas_call(
        _mla_post_kernel,
        out_shape=(jax.ShapeDtypeStruct((n, D_MODEL), F32), jax.ShapeDtypeStruct((n, D_MODEL), MM_DTYPE)),
        grid=(n // tile,),
        in_specs=[
            pl.BlockSpec((tile, D_MODEL), row), pl.BlockSpec((tile, D_MODEL), row), _mod_spec(cond_fn),
            _full(wo.shape), _full(lng.shape), _full(lnb.shape),
        ],
        out_specs=(pl.BlockSpec((tile, D_MODEL), row), pl.BlockSpec((tile, D_MODEL), row)),
        compiler_params=_params("parallel"),
        name="mla_post",
    )(o, x, mod, wo, lng, lnb)


def _pool_kernel(x_ref, xp_ref, xn_ref, mod_ref, w_ref, ps_ref, lng_ref, lnb_ref, x1_ref, h2_ref, ext_ref,
                 *, tile, tiles_per_seq):
    i = pl.program_id(0)
    pos = i % tiles_per_seq
    scale1 = 1.0 + mod_ref[0, 1:2, :]
    shift1 = mod_ref[0, 0:1, :]
    x = x_ref[...]
    keep_prev = (pos > 0).astype(F32)
    keep_next = (pos < tiles_per_seq - 1).astype(F32)
    ext_ref[0:POOL_HALO, :] = (xp_ref[...] * scale1 + shift1) * keep_prev
    ext_ref[POOL_HALO:POOL_HALO + tile, :] = x * scale1 + shift1
    ext_ref[POOL_HALO + tile:, :] = (xn_ref[...] * scale1 + shift1) * keep_next
    seq_len = tile * tiles_per_seq
    t = pos * tile + lax.broadcasted_iota(jnp.int32, (tile, POOL_GROUP_DIM), 0)
    ys = []
    for g, w in enumerate(POOL_WINDOWS):
        cs = slice(g * POOL_GROUP_DIM, (g + 1) * POOL_GROUP_DIM)
        tot = None
        for d in range(-(w // 2), w - w // 2):
            piece = ext_ref[POOL_HALO + d:POOL_HALO + d + tile, cs]
            tot = piece if tot is None else tot + piece
        lo = jnp.maximum(t - w // 2, 0)
        hi = jnp.minimum(t + (w - w // 2) - 1, seq_len - 1)
        cnt = (hi - lo + 1).astype(F32)
        diff = tot / cnt - ext_ref[POOL_HALO:POOL_HALO + tile, cs]
        ys.append(_nn(diff.astype(MM_DTYPE), w_ref[g]))
    y = jnp.concatenate(ys, axis=-1) * ps_ref[...]
    z = DEEPNORM_ALPHA * x + mod_ref[0, 2:3, :] * y
    _post_store(z, mod_ref, lng_ref, lnb_ref, x1_ref, h2_ref)


def _pool_layer(x, mod, cond_fn, seq_len, w, ps, lng, lnb, tile):
    n = x.shape[0]
    tps = seq_len // tile
    hb = tile // POOL_HALO
    nblk = n // POOL_HALO
    row = lambda i: (i, 0)
    return pl.pallas_call(
        functools.partial(_pool_kernel, tile=tile, tiles_per_seq=tps),
        out_shape=(jax.ShapeDtypeStruct((n, D_MODEL), F32), jax.ShapeDtypeStruct((n, D_MODEL), MM_DTYPE)),
        grid=(n // tile,),
        in_specs=[
            pl.BlockSpec((tile, D_MODEL), row),
            pl.BlockSpec((POOL_HALO, D_MODEL), lambda i: (jnp.maximum(i * hb - 1, 0), 0)),
            pl.BlockSpec((POOL_HALO, D_MODEL), lambda i: (jnp.minimum((i + 1) * hb, nblk - 1), 0)),
            _mod_spec(cond_fn), _full(w.shape), _full(ps.shape), _full(lng.shape), _full(lnb.shape),
        ],
        out_specs=(pl.BlockSpec((tile, D_MODEL), row), pl.BlockSpec((tile, D_MODEL), row)),
        scratch_shapes=[pltpu.VMEM((tile + 2 * POOL_HALO, D_MODEL), F32)],
        compiler_params=_params("parallel"),
        name="pool_layer",
    )(x, x, x, mod, w, ps, lng, lnb)


def _fnet_chan_kernel(x_ref, mod_ref, cc_ref, sc_ref, hc_ref, hs_ref):
    h = x_ref[...] * (1.0 + mod_ref[0, 1:2, :]) + mod_ref[0, 0:1, :]
    hb = h.astype(MM_DTYPE)
    for g in range(N_FOURIER_GROUPS):
        cs = slice(g * FOURIER_GROUP_DIM, (g + 1) * FOURIER_GROUP_DIM)
        hc_ref[:, cs] = _nn(hb[:, cs], cc_ref[...]).astype(hc_ref.dtype)
        hs_ref[:, cs] = _nn(hb[:, cs], sc_ref[...]).astype(hs_ref.dtype)


def _fnet_chan(x, mod, cond_fn, cc, sc, tile):
    n = x.shape[0]
    row = lambda i: (i, 0)
    return pl.pallas_call(
        _fnet_chan_kernel,
        out_shape=(jax.ShapeDtypeStruct((n, D_MODEL), MM_DTYPE), jax.ShapeDtypeStruct((n, D_MODEL), MM_DTYPE)),
        grid=(n // tile,),
        in_specs=[pl.BlockSpec((tile, D_MODEL), row), _mod_spec(cond_fn), _full(cc.shape), _full(sc.shape)],
        out_specs=(pl.BlockSpec((tile, D_MODEL), row), pl.BlockSpec((tile, D_MODEL), row)),
        compiler_params=_params("parallel"),
        name="fnet_chan",
    )(x, mod, cc, sc)


def _fnet_seq_kernel(cl_ref, sl_ref, hc_ref, hs_ref, x_ref, mod_ref, fw_ref, fb_ref, lng_ref, lnb_ref,
                     x1_ref, h2_ref, acc_ref):
    tt = pl.program_id(2)

    @pl.when(tt == 0)
    def _():
        acc_ref[...] = jnp.zeros_like(acc_ref)

    acc_ref[...] += _nn(cl_ref[...], hc_ref[...]) - _nn(sl_ref[...], hs_ref[...])

    @pl.when(tt == pl.num_programs(2) - 1)
    def _():
        y = _nn(acc_ref[...].astype(MM_DTYPE), fw_ref[...]) + fb_ref[...]
        z = DEEPNORM_ALPHA * x_ref[...] + mod_ref[0, 2:3, :] * y
        _post_store(z, mod_ref, lng_ref, lnb_ref, x1_ref, h2_ref)


def _fnet_seq(hc, hs, x, mod, per_seq, n_seq, seq_len, cl, sl, fw, fb, lng, lnb, tk, tt):
    n = x.shape[0]
    kb = seq_len // tk
    tb = seq_len // tt
    out_row = lambda b, k, t: (b * kb + k, 0)
    cond = (lambda b, k, t: (1 + b, 0, 0)) if per_seq else (lambda b, k, t: (0, 0, 0))
    return pl.pallas_call(
        _fnet_seq_kernel,
        out_shape=(jax.ShapeDtypeStruct((n, D_MODEL), F32), jax.ShapeDtypeStruct((n, D_MODEL), MM_DTYPE)),
        grid=(n_seq, kb, tb),
        in_specs=[
            pl.BlockSpec((tk, tt), lambda b, k, t: (k, t)),
            pl.BlockSpec((tk, tt), lambda b, k, t: (k, t)),
            pl.BlockSpec((tt, D_MODEL), lambda b, k, t: (b * tb + t, 0)),
            pl.BlockSpec((tt, D_MODEL), lambda b, k, t: (b * tb + t, 0)),
            pl.BlockSpec((tk, D_MODEL), out_row),
            pl.BlockSpec((1, 6, D_MODEL), cond),
            _full(fw.shape), _full(fb.shape), _full(lng.shape), _full(lnb.shape),
        ],
        out_specs=(pl.BlockSpec((tk, D_MODEL), out_row), pl.BlockSpec((tk, D_MODEL), out_row)),
        scratch_shapes=[pltpu.VMEM((tk, D_MODEL), F32)],
        compiler_params=_params("parallel", "parallel", "arbitrary"),
        name="fnet_seq",
    )(cl, sl, hc, hs, x, mod, fw, fb, lng, lnb)


def _top16(x):
    v = _sort_desc([x[SUBLANES * r:SUBLANES * (r + 1)] for r in range(x.shape[0] // SUBLANES)])
    rows = []
    for k in range(PEER_TOPK):
        m = jnp.max(v[0], axis=0, keepdims=True)
        rows.append(m)
        hit = v[0] == m
        v = [jnp.where(hit, v[d + 1], v[d]) for d in range(PEER_TOPK - 1 - k)]
    return rows


def _sort_desc(v):
    v = list(v)
    n = len(v)
    p = 1
    while p < n:
        k = p
        while k >= 1:
            for j in range(k % p, n - k, 2 * k):
                for i in range(min(k, n - j - k)):
                    if (i + j) // (2 * p) == (i + j + k) // (2 * p):
                        a, b = v[i + j], v[i + j + k]
                        v[i + j], v[i + j + k] = jnp.maximum(a, b), jnp.minimum(a, b)
            k //= 2
        p *= 2
    return v


def _stack8(rows, row8):
    out = jnp.broadcast_to(rows[0], row8.shape)
    for k in range(1, 8):
        out = jnp.where(row8 == k, rows[k], out)
    return out


def _peer_topk_kernel(h_ref, wqt_ref, sk_ref, e2_ref, th_ref, e1_ref, qt_ref, *, tile):
    qt_ref[...] = _nt(wqt_ref[...], h_ref[...]).astype(qt_ref.dtype)
    row8 = lax.broadcasted_iota(jnp.int32, (8, LANES), 0)

    def head(hd, carry):
        base = pl.multiple_of(hd * PEER_QUERY_DIM, PEER_QUERY_DIM)
        for lg in range(tile // LANES):
            ls = slice(lg * LANES, (lg + 1) * LANES)
            s1 = _nn(sk_ref[2 * hd], qt_ref[pl.ds(base, PEER_HALF), ls])
            s2 = _nn(sk_ref[2 * hd + 1], qt_ref[pl.ds(base + PEER_HALF, PEER_HALF), ls])
            t1 = _top16(s1)
            t2 = _top16(s2)
            t2a = _stack8(t2[:8], row8)
            t2b = _stack8(t2[8:], row8)
            t1b = _stack8(t1[8:], row8)
            cands = [t1[0] + t2a, t1[0] + t2b]
            for k in range(1, 8):
                cands.append(jnp.where(row8 < PEER_TOPK // (k + 1), t1[k] + t2a, -jnp.inf))
            cands.append(t1b + t2[0])
            c = jnp.concatenate(cands, axis=0)
            top = t1[0] + t2[0]
            z = jnp.zeros_like(top)
            tau = top
            for _ in range(PEER_TOPK):
                tau = jnp.max(c, axis=0, keepdims=True)
                z = z + jnp.exp(tau - top)
                c = jnp.where(c == tau, -jnp.inf, c)
            e2_of = lambda s: jnp.exp(s - t2[0]) * (0.5 / z)
            theta = jnp.full_like(s1, jnp.inf)
            for l in range(PEER_TOPK):
                theta = jnp.where(s1 + t2[l] >= tau, e2_of(t2[l]), theta)
            e2_ref[hd, lg] = e2_of(s2)
            th_ref[hd, lg] = theta
            e1_ref[hd, lg] = jnp.exp(s1 - t1[0])
        return carry

    lax.fori_loop(0, PEER_HEADS, head, 0)


def _peer_topk(h2, wqt, sk, tile):
    n = h2.shape[0]
    g = tile // LANES
    shp = jax.ShapeDtypeStruct((PEER_HEADS, n // LANES, PEER_N_KEYS, LANES), F32)
    spec = pl.BlockSpec((PEER_HEADS, g, PEER_N_KEYS, LANES), lambda i: (0, i, 0, 0))
    return pl.pallas_call(
        functools.partial(_peer_topk_kernel, tile=tile),
        out_shape=(shp, shp, shp),
        grid=(n // tile,),
        in_specs=[pl.BlockSpec((tile, D_MODEL), lambda i: (i, 0)), _full(wqt.shape), _full(sk.shape)],
        out_specs=(spec, spec, spec),
        scratch_shapes=[pltpu.VMEM((PEER_HEADS * PEER_QUERY_DIM, tile), MM_DTYPE)],
        compiler_params=_params("parallel"),
        name="peer_topk",
    )(h2, wqt, sk)


def _peer_main_kernel(h_ref, u_ref, vt_ref, e2_ref, th_ref, e1_ref, x1_ref, mod_ref, lng_ref, lnb_ref,
                      out_ref, at_ref, hm_ref, acc_ref, *, tile, eblk):
    j = pl.program_id(1)

    @pl.when(j == 0)
    def _():
        acc_ref[...] = jnp.zeros_like(acc_ref)

    at_ref[...] = _nt(u_ref[...], h_ref[...])
    for lg in range(tile // LANES):
        ls = slice(lg * LANES, (lg + 1) * LANES)
        for il in range(eblk // PEER_N_KEYS):
            rs = slice(il * PEER_N_KEYS, (il + 1) * PEER_N_KEYS)
            gate = None
            for hd in range(PEER_HEADS):
                e2 = e2_ref[hd, lg]
                w = jnp.where(e2 >= th_ref[hd, lg, il:il + 1, :], e1_ref[hd, lg, il:il + 1, :] * e2, 0.0)
                gate = w if gate is None else gate + w
            a = at_ref[rs, ls]
            hm_ref[rs, ls] = (gate * a * (1.0 + lax.erf(a * (2.0 ** -0.5)))).astype(hm_ref.dtype)
    acc_ref[...] += _nn(vt_ref[...], hm_ref[...])

    @pl.when(j == pl.num_programs(1) - 1)
    def _():
        f = acc_ref[...].T
        z = DEEPNORM_ALPHA * x1_ref[...] + mod_ref[0, 5:6, :] * f
        out_ref[...] = _layer_norm(z, lng_ref[...], lnb_ref[...])


def _peer_main(h2, u, vt, e2, th, e1, x1, mod, cond_fn, lng, lnb, tile, eblk):
    n = h2.shape[0]
    g = tile // LANES
    ik = eblk // PEER_N_KEYS
    row = lambda i, j: (i, 0)
    full_keys = pl.BlockSpec((PEER_HEADS, g, PEER_N_KEYS, LANES), lambda i, j: (0, i, 0, 0))
    slab_keys = pl.BlockSpec((PEER_HEADS, g, ik, LANES), lambda i, j: (0, i, j, 0))
    return pl.pallas_call(
        functools.partial(_peer_main_kernel, tile=tile, eblk=eblk),
        out_shape=jax.ShapeDtypeStruct((n, D_MODEL), F32),
        grid=(n // tile, PEER_N_EXPERTS // eblk),
        in_specs=[
            pl.BlockSpec((tile, D_MODEL), row),
            pl.BlockSpec((eblk, D_MODEL), lambda i, j: (j, 0)),
            pl.BlockSpec((D_MODEL, eblk), lambda i, j: (0, j)),
            full_keys, slab_keys, slab_keys,
            pl.BlockSpec((tile, D_MODEL), row),
            pl.BlockSpec((1, 6, D_MODEL), lambda i, j: (cond_fn(i), 0, 0)),
            pl.BlockSpec(lng.shape, lambda i, j: (0, 0)), pl.BlockSpec(lnb.shape, lambda i, j: (0, 0)),
        ],
        out_specs=pl.BlockSpec((tile, D_MODEL), row),
        scratch_shapes=[pltpu.VMEM((eblk, tile), F32), pltpu.VMEM((eblk, tile), MM_DTYPE),
                        pltpu.VMEM((D_MODEL, tile), F32)],
        compiler_params=_params("parallel", "arbitrary"),
        name="peer_main",
    )(h2, u, vt, e2, th, e1, x1, mod, lng, lnb)


def _peer_layer(h2, x1, mod, cond_fn, pw, lng, lnb, tile_topk, tile_main, eblk):
    e2, th, e1 = _peer_topk(h2, pw["wqt"], pw["sk"], tile_topk)
    return _peer_main(h2, pw["u"], pw["vt"], e2, th, e1, x1, mod, cond_fn, lng, lnb, tile_main, eblk)


def _pad_heads(w, n_used):
    k = w.shape[0]
    return jnp.pad(w, ((0, 0), (0, 0), (0, HEAD_PAD - n_used))).reshape(k, MLA_HEADS * HEAD_PAD)


def _rotate_half_cols(w):
    n = QK_ROPE_DIM // 4
    w4 = w.reshape(w.shape[:-1] + (2, 2, n))
    return jnp.stack([-w4[..., 1, :], w4[..., 0, :]], axis=-2).reshape(w.shape)


def _mla_weights(wdq, gq, wuq, wdkv, gkv, wukv):
    dq = QK_NOPE_DIM + QK_ROPE_DIM
    wuq3 = wuq.reshape(Q_LORA_RANK, MLA_HEADS, dq)
    pe = wuq3[..., QK_NOPE_DIM:]
    zeros_nope = jnp.zeros((Q_LORA_RANK, MLA_HEADS, QK_NOPE_DIM), F32)
    wuq_pad = _pad_heads(wuq3, dq)
    wuqs_pad = _pad_heads(jnp.concatenate([zeros_nope, _rotate_half_cols(pe)], axis=-1), dq)
    wkpe = wdkv[:, KV_LORA_RANK:]
    zk = jnp.zeros((D_MODEL, ROPE_OFF), F32)
    zt = jnp.zeros((D_MODEL, HEAD_PAD - ROPE_OFF - QK_ROPE_DIM), F32)
    wdkv_pad = jnp.concatenate([wdkv[:, :KV_LORA_RANK], zk, wkpe, zt, zk, _rotate_half_cols(wkpe), zt], axis=-1)
    wukv3 = wukv.reshape(KV_LORA_RANK, MLA_HEADS, QK_NOPE_DIM + V_HEAD_DIM)
    place = jnp.pad(jnp.eye(QK_ROPE_DIM, dtype=F32), ((0, 0), (ROPE_OFF, HEAD_PAD - ROPE_OFF - QK_ROPE_DIM)))
    return dict(
        wdq=wdq.astype(MM_DTYPE), gq=gq.reshape(1, -1), wuq=wuq_pad.astype(MM_DTYPE), wuqs=wuqs_pad.astype(MM_DTYPE),
        wdkv=wdkv_pad.astype(MM_DTYPE), gkv=gkv.reshape(1, -1),
        wukvk=_pad_heads(wukv3[..., :QK_NOPE_DIM], QK_NOPE_DIM).astype(MM_DTYPE),
        wukvv=wukv3[..., QK_NOPE_DIM:].reshape(KV_LORA_RANK, MLA_HEADS * V_HEAD_DIM).astype(MM_DTYPE),
        place=place.astype(MM_DTYPE),
    )


def _rope_tables(length):
    rows = length // GRID_W
    t = jnp.arange(rows * GRID_W)
    row = (t // GRID_W).astype(F32)
    col = (t % GRID_W).astype(F32)
    half = QK_ROPE_DIM // 2
    inv_freq = 1.0 / jnp.power(ROPE_THETA, jnp.arange(0, half, 2, dtype=F32) / half)
    n = QK_ROPE_DIM // 4
    ang = jnp.stack([row[:, None] * inv_freq, col[:, None] * inv_freq], axis=1)
    ang = jnp.broadcast_to(ang[:, :, None, :], (length, 2, 2, n)).reshape(length, QK_ROPE_DIM)
    pad = ((0, 0), (ROPE_OFF, HEAD_PAD - ROPE_OFF - QK_ROPE_DIM))
    cos = jnp.pad(jnp.cos(ang), pad, constant_values=1.0)
    sin = jnp.pad(jnp.sin(ang), pad)
    return cos, sin


def _dft_tables(length, scale):
    k = jnp.arange(length, dtype=jnp.int32)
    kt = (k[:, None] * k[None, :]) % length
    ang = kt.astype(F32) * (2.0 * math.pi / length)
    return (jnp.cos(ang) * scale).astype(MM_DTYPE), (jnp.sin(ang) * scale).astype(MM_DTYPE)


def kernel(x_prompt, x_sample, cache_ckv, cache_kpe, c, c_ctx, ada_w, ada_b, ln_mix_g, ln_mix_b, ln_ffn_g, ln_ffn_b, mla_wdq, mla_q_norm_g, mla_wuq, mla_wdkv, mla_kv_norm_g, mla_wukv, mla_wo, pool_w, pool_scale, fnet_w, fnet_b, peer_wq, peer_subkeys, peer_u, peer_v):
    nb_p, len_p, _ = x_prompt.shape
    nb_s, len_s, _ = x_sample.shape
    assert 1 + nb_s <= N_COND_ROWS
    cond = jnp.zeros((N_COND_ROWS, D_MODEL), F32).at[0].set(c_ctx).at[1:1 + nb_s].set(c)
    mods = _adaln(cond, ada_w, ada_b)

    streams = [
        dict(x=x_prompt.reshape(nb_p * len_p, D_MODEL), n_seq=nb_p, seq_len=len_p, per_seq=False),
        dict(x=x_sample.reshape(nb_s * len_s, D_MODEL), n_seq=nb_s, seq_len=len_s, per_seq=True),
    ]
    tile = 256
    tile_main = 1024
    eblk = 1024
    cos_s, sin_s = _rope_tables(len_s)
    ident_cos = jnp.ones((tile, HEAD_PAD), F32)
    ident_sin = jnp.zeros((tile, HEAD_PAD), F32)
    new_ckv, new_kpe = [], []

    for i in range(DEPTH):
        kind, j = i % N_MIXERS, i // N_MIXERS
        mod = mods[i]
        lng, lnb = ln_mix_g[i].reshape(1, -1), ln_mix_b[i].reshape(1, -1)
        pw = dict(
            wqt=peer_wq[i].T.astype(MM_DTYPE),
            sk=peer_subkeys[i].reshape(PEER_HEADS * 2, PEER_N_KEYS, PEER_HALF).astype(MM_DTYPE),
            u=peer_u[i].astype(MM_DTYPE),
            vt=peer_v[i].T.astype(MM_DTYPE),
        )
        if kind == 0:
            mw = _mla_weights(mla_wdq[j], mla_q_norm_g[j], mla_wuq[j], mla_wdkv[j], mla_kv_norm_g[j], mla_wukv[j])
            wo = mla_wo[j].astype(MM_DTYPE)
            ctx = _mla_ctx(cache_ckv, cache_kpe, j, mw)
        elif kind == 2:
            cc, sc = _dft_tables(FOURIER_GROUP_DIM, FOURIER_GROUP_DIM ** -0.5)
        for s in streams:
            x, seq_len, n_seq = s["x"], s["seq_len"], s["n_seq"]
            cond_fn = _cond_fn(s["per_seq"], tile, seq_len)
            if kind == 0:
                if s["per_seq"]:
                    tps = seq_len // tile
                    q, k, v, _, _ = _mla_proj(x, mod, cond_fn, cos_s, sin_s, lambda t, tps=tps: t % tps, mw, tile)
                    o = _attention(q, k, v, n_seq, seq_len, min(256, seq_len), ctx=ctx)
                else:
                    q, k, v, ckv, kpe = _mla_proj(x, mod, cond_fn, ident_cos, ident_sin, lambda t: 0, mw, tile)
                    o = _attention(q, k, v, n_seq, seq_len, min(256, seq_len))
                    new_ckv.append(ckv.reshape(n_seq, seq_len, KV_LORA_RANK))
                    new_kpe.append(kpe.reshape(n_seq, seq_len, QK_ROPE_DIM))
                x1, h2 = _mla_post(o, x, mod, cond_fn, wo, lng, lnb, tile)
            elif kind == 1:
                x1, h2 = _pool_layer(x, mod, cond_fn, seq_len, pool_w[j].astype(MM_DTYPE),
                                     pool_scale[j].reshape(1, -1), lng, lnb, tile)
            else:
                hc, hs = _fnet_chan(x, mod, cond_fn, cc, sc, tile)
                cl, sl = _dft_tables(seq_len, seq_len ** -0.5)
                x1, h2 = _fnet_seq(hc, hs, x, mod, s["per_seq"], n_seq, seq_len, cl, sl,
                                   fnet_w[j].astype(MM_DTYPE), fnet_b[j].reshape(1, -1), lng, lnb,
                                   min(512, seq_len), min(1024, seq_len))
            cond_main = _cond_fn(s["per_seq"], min(tile_main, x.shape[0]), seq_len)
            s["x"] = _peer_layer(h2, x1, mod, cond_main, pw, ln_ffn_g[i].reshape(1, -1), ln_ffn_b[i].reshape(1, -1),
                                 tile, min(tile_main, x.shape[0]), eblk)

    y_prompt = streams[0]["x"].reshape(nb_p, len_p, D_MODEL)
    y_sample = streams[1]["x"].reshape(nb_s, len_s, D_MODEL)
    return (y_prompt, y_sample, jnp.stack(new_ckv, axis=1), jnp.stack(new_kpe, axis=1))
```

```python
import functools
import math

import jax
import jax.numpy as jnp
from jax import lax
from jax.experimental import pallas as pl
from jax.experimental.pallas import tpu as pltpu

F32 = jnp.float32
MM_DTYPE = jnp.bfloat16

D_MODEL = 1024
DEPTH = 4
GRID_W = 64
N_MIXERS = 3

MLA_HEADS = 16
QK_NOPE_DIM = 64
QK_ROPE_DIM = 32
V_HEAD_DIM = 64
Q_LORA_RANK = 384
KV_LORA_RANK = 256
ROPE_THETA = 10000.0
HEAD_PAD = 128
ROPE_OFF = QK_NOPE_DIM

POOL_WINDOWS = (2, 4, 8, 16)
N_POOL_GROUPS = 4
POOL_GROUP_DIM = D_MODEL // N_POOL_GROUPS
POOL_HALO = 8

N_FOURIER_GROUPS = 4
FOURIER_GROUP_DIM = D_MODEL // N_FOURIER_GROUPS

PEER_HEADS = 8
PEER_N_KEYS = 128
PEER_N_EXPERTS = PEER_N_KEYS * PEER_N_KEYS
PEER_QUERY_DIM = 256
PEER_HALF = PEER_QUERY_DIM // 2
PEER_TOPK = 16

DEEPNORM_ALPHA = (2 * DEPTH) ** 0.25
LN_EPS = 1e-5
RMS_EPS = 1e-6

LANES = 128
SUBLANES = 8
N_COND_ROWS = 16
VMEM_LIMIT = 56 * 1024 * 1024


def _nn(a, b):
    return jnp.dot(a, b, preferred_element_type=F32)


def _nt(a, b):
    return lax.dot_general(a, b, (((1,), (1,)), ((), ())), preferred_element_type=F32)


def _params(*sem, flags=None):
    return pltpu.CompilerParams(dimension_semantics=sem, vmem_limit_bytes=VMEM_LIMIT, flags=flags)


def _layer_norm(z, g, b):
    mu = jnp.mean(z, axis=-1, keepdims=True)
    d = z - mu
    var = jnp.mean(d * d, axis=-1, keepdims=True)
    return d * lax.rsqrt(var + LN_EPS) * g + b


def _rms_norm(x, g):
    return x * lax.rsqrt(jnp.mean(x * x, axis=-1, keepdims=True) + RMS_EPS) * g


def _full(shape):
    n = len(shape)
    return pl.BlockSpec(shape, lambda *_: (0,) * n)


def _mod_spec(cond_fn):
    return pl.BlockSpec((1, 6, D_MODEL), lambda i, *_: (cond_fn(i), 0, 0))


def _cond_fn(per_seq, tile, seq_len):
    if not per_seq:
        return lambda i: 0
    return lambda i: 1 + (i * tile) // seq_len


def _adaln_kernel(cond_ref, w_ref, b_ref, o_ref):
    c = cond_ref[...]
    c = c / (1.0 + jnp.exp(-c))
    o_ref[0] = jnp.dot(c, w_ref[0], preferred_element_type=F32,
                       precision=lax.Precision.HIGHEST) + b_ref[0]


def _adaln(cond, ada_w, ada_b):
    nb = 1536
    out = pl.pallas_call(
        _adaln_kernel,
        out_shape=jax.ShapeDtypeStruct((DEPTH, N_COND_ROWS, 6 * D_MODEL), F32),
        grid=(DEPTH, 6 * D_MODEL // nb),
        in_specs=[
            pl.BlockSpec((N_COND_ROWS, D_MODEL), lambda i, j: (0, 0)),
            pl.BlockSpec((1, D_MODEL, nb), lambda i, j: (i, 0, j)),
            pl.BlockSpec((1, 1, nb), lambda i, j: (i, 0, j)),
        ],
        out_specs=pl.BlockSpec((1, N_COND_ROWS, nb), lambda i, j: (i, 0, j)),
        compiler_params=_params("parallel", "parallel"),
        name="adaln",
    )(cond, ada_w, ada_b.reshape(DEPTH, 1, 6 * D_MODEL))
    return out.reshape(DEPTH, N_COND_ROWS, 6, D_MODEL)


def _mla_proj_kernel(x_ref, mod_ref, cos_ref, sin_ref, wdq_ref, gq_ref, wuq_ref, wuqs_ref,
                     wdkv_ref, gkv_ref, wukvk_ref, wukvv_ref,
                     q_ref, k_ref, v_ref, ckv_ref, kpe_ref):
    h = x_ref[...] * (1.0 + mod_ref[0, 1:2, :]) + mod_ref[0, 0:1, :]
    hb = h.astype(MM_DTYPE)
    cq = _rms_norm(_nn(hb, wdq_ref[...]), gq_ref[...]).astype(MM_DTYPE)
    q = _nn(cq, wuq_ref[...])
    qs = _nn(cq, wuqs_ref[...])
    kv = _nn(hb, wdkv_ref[...])
    ckv = _rms_norm(kv[:, :KV_LORA_RANK], gkv_ref[...])
    kpe = kv[:, KV_LORA_RANK:KV_LORA_RANK + HEAD_PAD]
    kpes = kv[:, KV_LORA_RANK + HEAD_PAD:]
    cos = cos_ref[...]
    sin = sin_ref[...]
    kpe_rot = kpe * cos + kpes * sin
    ckvb = ckv.astype(MM_DTYPE)
    kn = _nn(ckvb, wukvk_ref[...])
    scale = (QK_NOPE_DIM + QK_ROPE_DIM) ** -0.5 * math.log2(math.e)
    for hd in range(MLA_HEADS):
        sl = slice(hd * HEAD_PAD, (hd + 1) * HEAD_PAD)
        q_ref[:, sl] = ((q[:, sl] * cos + qs[:, sl] * sin) * scale).astype(q_ref.dtype)
        k_ref[:, sl] = (kn[:, sl] + kpe_rot).astype(k_ref.dtype)
    v_ref[0], v_ref[1] = _split_values(_nn(ckvb, wukvv_ref[...]), v_ref.dtype)
    ckv_ref[...] = ckv
    kpe_ref[...] = kpe[:, ROPE_OFF:ROPE_OFF + QK_ROPE_DIM]


def _mla_proj(x, mod, cond_fn, cos, sin, pos_fn, w, tile):
    n = x.shape[0]
    hp = MLA_HEADS * HEAD_PAD
    hv = MLA_HEADS * V_HEAD_DIM
    row = lambda i: (i, 0)
    return pl.pallas_call(
        _mla_proj_kernel,
        out_shape=(
            jax.ShapeDtypeStruct((n, hp), MM_DTYPE),
            jax.ShapeDtypeStruct((n, hp), MM_DTYPE),
            jax.ShapeDtypeStruct((2, n, hv), MM_DTYPE),
            jax.ShapeDtypeStruct((n, KV_LORA_RANK), F32),
            jax.ShapeDtypeStruct((n, QK_ROPE_DIM), F32),
        ),
        grid=(n // tile,),
        in_specs=[
            pl.BlockSpec((tile, D_MODEL), row),
            _mod_spec(cond_fn),
            pl.BlockSpec((tile, HEAD_PAD), lambda i: (pos_fn(i), 0)),
            pl.BlockSpec((tile, HEAD_PAD), lambda i: (pos_fn(i), 0)),
            _full(w["wdq"].shape), _full(w["gq"].shape), _full(w["wuq"].shape), _full(w["wuqs"].shape),
            _full(w["wdkv"].shape), _full(w["gkv"].shape), _full(w["wukvk"].shape), _full(w["wukvv"].shape),
        ],
        out_specs=(
            pl.BlockSpec((tile, hp), row),
            pl.BlockSpec((tile, hp), row),
            pl.BlockSpec((2, tile, hv), lambda i: (0, i, 0)),
            pl.BlockSpec((tile, KV_LORA_RANK), row),
            pl.BlockSpec((tile, QK_ROPE_DIM), row),
        ),
        compiler_params=_params("parallel"),
        name="mla_proj",
    )(x, mod, cos, sin, w["wdq"], w["gq"], w["wuq"], w["wuqs"], w["wdkv"], w["gkv"], w["wukvk"], w["wukvv"])


def _mla_ctx_kernel(ckv_ref, kpe_ref, wukvk_ref, wukvv_ref, place_ref, k_ref, v_ref):
    ckvb = ckv_ref[...].astype(MM_DTYPE)
    kn = _nn(ckvb, wukvk_ref[...])
    kpe = _nn(kpe_ref[...].astype(MM_DTYPE), place_ref[...])
    for hd in range(MLA_HEADS):
        sl = slice(hd * HEAD_PAD, (hd + 1) * HEAD_PAD)
        k_ref[:, sl] = (kn[:, sl] + kpe).astype(k_ref.dtype)
    v_ref[0], v_ref[1] = _split_values(_nn(ckvb, wukvv_ref[...]), v_ref.dtype)


def _mla_ctx(cache_ckv, cache_kpe, j, w):
    nb, _, past, _ = cache_ckv.shape
    hp = MLA_HEADS * HEAD_PAD
    hv = MLA_HEADS * V_HEAD_DIM
    return pl.pallas_call(
        _mla_ctx_kernel,
        out_shape=(jax.ShapeDtypeStruct((nb * past, hp), MM_DTYPE),
                   jax.ShapeDtypeStruct((2, nb * past, hv), MM_DTYPE)),
        grid=(nb,),
        in_specs=[
            pl.BlockSpec((None, None, past, KV_LORA_RANK), lambda b: (b, j, 0, 0)),
            pl.BlockSpec((None, None, past, QK_ROPE_DIM), lambda b: (b, j, 0, 0)),
            _full(w["wukvk"].shape), _full(w["wukvv"].shape), _full(w["place"].shape),
        ],
        out_specs=(pl.BlockSpec((past, hp), lambda b: (b, 0)),
                   pl.BlockSpec((2, past, hv), lambda b: (0, b, 0))),
        compiler_params=_params("parallel"),
        name="mla_ctx",
    )(cache_ckv, cache_kpe, w["wukvk"], w["wukvv"], w["place"])


def _attn_kernel(*refs, has_ctx):
    if has_ctx:
        q_ref, k_ref, v_ref, kc_ref, vc_ref, o_ref = refs
    else:
        q_ref, k_ref, v_ref, o_ref = refs
    scores = []
    for hh in range(2):
        sl = slice(hh * HEAD_PAD, (hh + 1) * HEAD_PAD)
        q = q_ref[:, sl]
        s = _nt(q, k_ref[:, sl])
        m = jnp.max(s, axis=-1, keepdims=True)
        sc = None
        if has_ctx:
            sc = _nt(q, kc_ref[:, sl])
            m = jnp.maximum(m, jnp.max(sc, axis=-1, keepdims=True))
        scores.append((s, sc, m))
    probs = []
    for s, sc, m in scores:
        p = jnp.exp2(s - m).astype(MM_DTYPE)
        pc = jnp.exp2(sc - m).astype(MM_DTYPE) if has_ctx else None
        probs.append((p, pc))
    outs = []
    for hh, (p, pc) in enumerate(probs):
        o = _nn(p, v_ref[hh])
        if has_ctx:
            o = o + _nn(pc, vc_ref[hh])
        ol = _ones_lane(hh)
        outs.append(o * (1.0 / o[:, ol:ol + 1]))
    lane = lax.broadcasted_iota(jnp.int32, (1, 2 * V_HEAD_DIM), 1)
    o_ref[...] = jnp.where(lane < V_HEAD_DIM, outs[0], outs[1]).astype(o_ref.dtype)


def _ones_lane(hh):
    return V_HEAD_DIM if hh == 0 else 0


def _split_values(v, dtype):
    lane = lax.broadcasted_iota(jnp.int32, (1, 2 * V_HEAD_DIM), 1)
    parts = []
    for hh in range(2):
        own = (lane < V_HEAD_DIM) if hh == 0 else (lane >= V_HEAD_DIM)
        fill = jnp.where(lane == _ones_lane(hh), 1.0, 0.0)
        blocks = [jnp.where(own, v[:, p * 2 * V_HEAD_DIM:(p + 1) * 2 * V_HEAD_DIM], fill)
                  for p in range(MLA_HEADS // 2)]
        parts.append(jnp.concatenate(blocks, axis=-1).astype(dtype))
    return parts


def _attention(q, k, v, n_seq, seq_len, tq, ctx=None):
    n = q.shape[0]
    hpairs = MLA_HEADS // 2
    qb = seq_len // tq
    in_specs = [
        pl.BlockSpec((tq, 2 * HEAD_PAD), lambda b, h, i: (b * qb + i, h)),
        pl.BlockSpec((seq_len, 2 * HEAD_PAD), lambda b, h, i: (b, h)),
        pl.BlockSpec((2, seq_len, 2 * V_HEAD_DIM), lambda b, h, i: (0, b, h)),
    ]
    args = [q, k, v]
    if ctx is not None:
        kc, vc = ctx
        past = kc.shape[0] // n_seq
        in_specs += [
            pl.BlockSpec((past, 2 * HEAD_PAD), lambda b, h, i: (b, h)),
            pl.BlockSpec((2, past, 2 * V_HEAD_DIM), lambda b, h, i: (0, b, h)),
        ]
        args += [kc, vc]
    return pl.pallas_call(
        functools.partial(_attn_kernel, has_ctx=ctx is not None),
        out_shape=jax.ShapeDtypeStruct((n, MLA_HEADS * V_HEAD_DIM), MM_DTYPE),
        grid=(n_seq, hpairs, qb),
        in_specs=in_specs,
        out_specs=pl.BlockSpec((tq, 2 * V_HEAD_DIM), lambda b, h, i: (b * qb + i, h)),
        compiler_params=_params("parallel", "parallel", "arbitrary"),
        name="mla_attn",
    )(*args)


def _post_store(z, mod_ref, lng_ref, lnb_ref, x1_ref, h2_ref):
    x1 = _layer_norm(z, lng_ref[...], lnb_ref[...])
    x1_ref[...] = x1
    h2_ref[...] = (x1 * (1.0 + mod_ref[0, 4:5, :]) + mod_ref[0, 3:4, :]).astype(h2_ref.dtype)


def _mla_post_kernel(o_ref, x_ref, mod_ref, wo_ref, lng_ref, lnb_ref, x1_ref, h2_ref):
    y = _nn(o_ref[...], wo_ref[...])
    z = DEEPNORM_ALPHA * x_ref[...] + mod_ref[0, 2:3, :] * y
    _post_store(z, mod_ref, lng_ref, lnb_ref, x1_ref, h2_ref)


def _mla_post(o, x, mod, cond_fn, wo, lng, lnb, tile):
    n = x.shape[0]
    row = lambda i: (i, 0)
    return pl.pallas_call(
        _mla_post_kernel,
        out_shape=(jax.ShapeDtypeStruct((n, D_MODEL), F32), jax.ShapeDtypeStruct((n, D_MODEL), MM_DTYPE)),
        grid=(n // tile,),
        in_specs=[
            pl.BlockSpec((tile, D_MODEL), row), pl.BlockSpec((tile, D_MODEL), row), _mod_spec(cond_fn),
            _full(wo.shape), _full(lng.shape), _full(lnb.shape),
        ],
        out_specs=(pl.BlockSpec((tile, D_MODEL), row), pl.BlockSpec((tile, D_MODEL), row)),
        compiler_params=_params("parallel"),
        name="mla_post",
    )(o, x, mod, wo, lng, lnb)


def _pool_kernel(x_ref, xp_ref, xn_ref, mod_ref, w_ref, ps_ref, lng_ref, lnb_ref, x1_ref, h2_ref, ext_ref,
                 *, tile, tiles_per_seq):
    i = pl.program_id(0)
    pos = i % tiles_per_seq
    scale1 = 1.0 + mod_ref[0, 1:2, :]
    shift1 = mod_ref[0, 0:1, :]
    x = x_ref[...]
    keep_prev = (pos > 0).astype(F32)
    keep_next = (pos < tiles_per_seq - 1).astype(F32)
    ext_ref[0:POOL_HALO, :] = (xp_ref[...] * scale1 + shift1) * keep_prev
    ext_ref[POOL_HALO:POOL_HALO + tile, :] = x * scale1 + shift1
    ext_ref[POOL_HALO + tile:, :] = (xn_ref[...] * scale1 + shift1) * keep_next
    seq_len = tile * tiles_per_seq
    t = pos * tile + lax.broadcasted_iota(jnp.int32, (tile, POOL_GROUP_DIM), 0)
    ys = []
    for g, w in enumerate(POOL_WINDOWS):
        cs = slice(g * POOL_GROUP_DIM, (g + 1) * POOL_GROUP_DIM)
        tot = None
        for d in range(-(w // 2), w - w // 2):
            piece = ext_ref[POOL_HALO + d:POOL_HALO + d + tile, cs]
            tot = piece if tot is None else tot + piece
        lo = jnp.maximum(t - w // 2, 0)
        hi = jnp.minimum(t + (w - w // 2) - 1, seq_len - 1)
        cnt = (hi - lo + 1).astype(F32)
        diff = tot / cnt - ext_ref[POOL_HALO:POOL_HALO + tile, cs]
        ys.append(_nn(diff.astype(MM_DTYPE), w_ref[g]))
    y = jnp.concatenate(ys, axis=-1) * ps_ref[...]
    z = DEEPNORM_ALPHA * x + mod_ref[0, 2:3, :] * y
    _post_store(z, mod_ref, lng_ref, lnb_ref, x1_ref, h2_ref)


def _pool_layer(x, mod, cond_fn, seq_len, w, ps, lng, lnb, tile):
    n = x.shape[0]
    tps = seq_len // tile
    hb = tile // POOL_HALO
    nblk = n // POOL_HALO
    row = lambda i: (i, 0)
    return pl.pallas_call(
        functools.partial(_pool_kernel, tile=tile, tiles_per_seq=tps),
        out_shape=(jax.ShapeDtypeStruct((n, D_MODEL), F32), jax.ShapeDtypeStruct((n, D_MODEL), MM_DTYPE)),
        grid=(n // tile,),
        in_specs=[
            pl.BlockSpec((tile, D_MODEL), row),
            pl.BlockSpec((POOL_HALO, D_MODEL), lambda i: (jnp.maximum(i * hb - 1, 0), 0)),
            pl.BlockSpec((POOL_HALO, D_MODEL), lambda i: (jnp.minimum((i + 1) * hb, nblk - 1), 0)),
            _mod_spec(cond_fn), _full(w.shape), _full(ps.shape), _full(lng.shape), _full(lnb.shape),
        ],
        out_specs=(pl.BlockSpec((tile, D_MODEL), row), pl.BlockSpec((tile, D_MODEL), row)),
        scratch_shapes=[pltpu.VMEM((tile + 2 * POOL_HALO, D_MODEL), F32)],
        compiler_params=_params("parallel"),
        name="pool_layer",
    )(x, x, x, mod, w, ps, lng, lnb)


def _fnet_chan_kernel(x_ref, mod_ref, cc_ref, sc_ref, hc_ref, hs_ref):
    h = x_ref[...] * (1.0 + mod_ref[0, 1:2, :]) + mod_ref[0, 0:1, :]
    hb = h.astype(MM_DTYPE)
    for g in range(N_FOURIER_GROUPS):
        cs = slice(g * FOURIER_GROUP_DIM, (g + 1) * FOURIER_GROUP_DIM)
        hc_ref[:, cs] = _nn(hb[:, cs], cc_ref[...]).astype(hc_ref.dtype)
        hs_ref[:, cs] = _nn(hb[:, cs], sc_ref[...]).astype(hs_ref.dtype)


def _fnet_chan(x, mod, cond_fn, cc, sc, tile):
    n = x.shape[0]
    row = lambda i: (i, 0)
    return pl.pallas_call(
        _fnet_chan_kernel,
        out_shape=(jax.ShapeDtypeStruct((n, D_MODEL), MM_DTYPE), jax.ShapeDtypeStruct((n, D_MODEL), MM_DTYPE)),
        grid=(n // tile,),
        in_specs=[pl.BlockSpec((tile, D_MODEL), row), _mod_spec(cond_fn), _full(cc.shape), _full(sc.shape)],
        out_specs=(pl.BlockSpec((tile, D_MODEL), row), pl.BlockSpec((tile, D_MODEL), row)),
        compiler_params=_params("parallel"),
        name="fnet_chan",
    )(x, mod, cc, sc)


def _fnet_seq_kernel(cl_ref, sl_ref, hc_ref, hs_ref, x_ref, mod_ref, fw_ref, fb_ref, lng_ref, lnb_ref,
                     x1_ref, h2_ref, acc_ref):
    tt = pl.program_id(2)

    @pl.when(tt == 0)
    def _():
        acc_ref[...] = jnp.zeros_like(acc_ref)

    acc_ref[...] += _nn(cl_ref[...], hc_ref[...]) - _nn(sl_ref[...], hs_ref[...])

    @pl.when(tt == pl.num_programs(2) - 1)
    def _():
        y = _nn(acc_ref[...].astype(MM_DTYPE), fw_ref[...]) + fb_ref[...]
        z = DEEPNORM_ALPHA * x_ref[...] + mod_ref[0, 2:3, :] * y
        _post_store(z, mod_ref, lng_ref, lnb_ref, x1_ref, h2_ref)


def _fnet_seq(hc, hs, x, mod, per_seq, n_seq, seq_len, cl, sl, fw, fb, lng, lnb, tk, tt):
    n = x.shape[0]
    kb = seq_len // tk
    tb = seq_len // tt
    out_row = lambda b, k, t: (b * kb + k, 0)
    cond = (lambda b, k, t: (1 + b, 0, 0)) if per_seq else (lambda b, k, t: (0, 0, 0))
    return pl.pallas_call(
        _fnet_seq_kernel,
        out_shape=(jax.ShapeDtypeStruct((n, D_MODEL), F32), jax.ShapeDtypeStruct((n, D_MODEL), MM_DTYPE)),
        grid=(n_seq, kb, tb),
        in_specs=[
            pl.BlockSpec((tk, tt), lambda b, k, t: (k, t)),
            pl.BlockSpec((tk, tt), lambda b, k, t: (k, t)),
            pl.BlockSpec((tt, D_MODEL), lambda b, k, t: (b * tb + t, 0)),
            pl.BlockSpec((tt, D_MODEL), lambda b, k, t: (b * tb + t, 0)),
            pl.BlockSpec((tk, D_MODEL), out_row),
            pl.BlockSpec((1, 6, D_MODEL), cond),
            _full(fw.shape), _full(fb.shape), _full(lng.shape), _full(lnb.shape),
        ],
        out_specs=(pl.BlockSpec((tk, D_MODEL), out_row), pl.BlockSpec((tk, D_MODEL), out_row)),
        scratch_shapes=[pltpu.VMEM((tk, D_MODEL), F32)],
        compiler_params=_params("parallel", "parallel", "arbitrary"),
        name="fnet_seq",
    )(cl, sl, hc, hs, x, mod, fw, fb, lng, lnb)


def _top16(x):
    v = _sort_desc([x[SUBLANES * r:SUBLANES * (r + 1)] for r in range(x.shape[0] // SUBLANES)])
    rows = []
    for k in range(PEER_TOPK):
        m = jnp.max(v[0], axis=0, keepdims=True)
        rows.append(m)
        hit = v[0] == m
        v = [jnp.where(hit, v[d + 1], v[d]) for d in range(PEER_TOPK - 1 - k)]
    return rows


def _sort_desc(v):
    v = list(v)
    n = len(v)
    p = 1
    while p < n:
        k = p
        while k >= 1:
            for j in range(k % p, n - k, 2 * k):
                for i in range(min(k, n - j - k)):
                    if (i + j) // (2 * p) == (i + j + k) // (2 * p):
                        a, b = v[i + j], v[i + j + k]
                        v[i + j], v[i + j + k] = jnp.maximum(a, b), jnp.minimum(a, b)
            k //= 2
        p *= 2
    return v


def _stack8(rows, row8):
    out = jnp.broadcast_to(rows[0], row8.shape)
    for k in range(1, 8):
        out = jnp.where(row8 == k, rows[k], out)
    return out


def _peer_topk_kernel(h_ref, wqt_ref, sk_ref, e2_ref, th_ref, e1_ref, qt_ref, *, tile):
    qt_ref[...] = _nt(wqt_ref[...], h_ref[...]).astype(qt_ref.dtype)
    row8 = lax.broadcasted_iota(jnp.int32, (8, LANES), 0)

    def head(hd, carry):
        base = pl.multiple_of(hd * PEER_QUERY_DIM, PEER_QUERY_DIM)
        for lg in range(tile // LANES):
            ls = slice(lg * LANES, (lg + 1) * LANES)
            s1 = _nn(sk_ref[2 * hd], qt_ref[pl.ds(base, PEER_HALF), ls])
            s2 = _nn(sk_ref[2 * hd + 1], qt_ref[pl.ds(base + PEER_HALF, PEER_HALF), ls])
            t1 = _top16(s1)
            t2 = _top16(s2)
            t1a = _stack8(t1[:8], row8)
            lists = [t1a + t2[0]] + [jnp.where(row8 < PEER_TOPK // (l + 1), t1a + t2[l], -jnp.inf)
                                     for l in range(1, PEER_TOPK)]
            tail = _stack8(t1[8:], row8) + t2[0]
            top = t1[0] + t2[0]
            z = jnp.zeros_like(top)
            tau = top
            for j in range(PEER_TOPK):
                tau = jnp.max(jnp.maximum(lists[0], tail), axis=0, keepdims=True)
                z = z + jnp.exp(tau - top)
                if j < PEER_TOPK - 1:
                    hit = lists[0] == tau
                    tail = jnp.where(tail == tau, -jnp.inf, tail)
                    lists = [jnp.where(hit, lists[d + 1], lists[d]) for d in range(len(lists) - 1)]
            e2_of = lambda s: jnp.exp(s - t2[0]) * (0.5 / z)
            theta = jnp.full_like(s1, jnp.inf)
            for l in range(PEER_TOPK // 2):
                theta = jnp.where(s1 + t2[l] >= tau, e2_of(t2[l]), theta)
            best = jnp.full_like(top, jnp.inf)
            for l in range(PEER_TOPK // 2, PEER_TOPK):
                best = jnp.where(t1[0] + t2[l] >= tau, e2_of(t2[l]), best)
            theta = jnp.where(s1 == t1[0], jnp.minimum(theta, best), theta)
            e2_ref[hd, lg] = e2_of(s2)
            th_ref[hd, lg] = theta
            e1_ref[hd, lg] = jnp.exp(s1 - t1[0])
        return carry

    lax.fori_loop(0, PEER_HEADS, head, 0)


def _peer_topk(h2, wqt, sk, tile):
    n = h2.shape[0]
    g = tile // LANES
    shp = jax.ShapeDtypeStruct((PEER_HEADS, n // LANES, PEER_N_KEYS, LANES), F32)
    spec = pl.BlockSpec((PEER_HEADS, g, PEER_N_KEYS, LANES), lambda i: (0, i, 0, 0))
    return pl.pallas_call(
        functools.partial(_peer_topk_kernel, tile=tile),
        out_shape=(shp, shp, shp),
        grid=(n // tile,),
        in_specs=[pl.BlockSpec((tile, D_MODEL), lambda i: (i, 0)), _full(wqt.shape), _full(sk.shape)],
        out_specs=(spec, spec, spec),
        scratch_shapes=[pltpu.VMEM((PEER_HEADS * PEER_QUERY_DIM, tile), MM_DTYPE)],
        compiler_params=_params("parallel"),
        name="peer_topk",
    )(h2, wqt, sk)


def _peer_main_kernel(h_ref, u_ref, vt_ref, e2_ref, th_ref, e1_ref, x1_ref, mod_ref, lng_ref, lnb_ref,
                      out_ref, at_ref, hm_ref, acc_ref, *, tile, eblk):
    j = pl.program_id(1)

    @pl.when(j == 0)
    def _():
        acc_ref[...] = jnp.zeros_like(acc_ref)

    at_ref[...] = _nt(u_ref[...], h_ref[...])
    for lg in range(tile // LANES):
        ls = slice(lg * LANES, (lg + 1) * LANES)
        for il in range(eblk // PEER_N_KEYS):
            rs = slice(il * PEER_N_KEYS, (il + 1) * PEER_N_KEYS)
            gate = None
            for hd in range(PEER_HEADS):
                e2 = e2_ref[hd, lg]
                w = jnp.where(e2 >= th_ref[hd, lg, il:il + 1, :], e1_ref[hd, lg, il:il + 1, :] * e2, 0.0)
                gate = w if gate is None else gate + w
            a = at_ref[rs, ls]
            hm_ref[rs, ls] = (gate * a * (1.0 + lax.erf(a * (2.0 ** -0.5)))).astype(hm_ref.dtype)
    acc_ref[...] += _nn(vt_ref[...], hm_ref[...])

    @pl.when(j == pl.num_programs(1) - 1)
    def _():
        f = acc_ref[...].T
        z = DEEPNORM_ALPHA * x1_ref[...] + mod_ref[0, 5:6, :] * f
        out_ref[...] = _layer_norm(z, lng_ref[...], lnb_ref[...])


def _peer_main(h2, u, vt, e2, th, e1, x1, mod, cond_fn, lng, lnb, tile, eblk):
    n = h2.shape[0]
    g = tile // LANES
    ik = eblk // PEER_N_KEYS
    row = lambda i, j: (i, 0)
    full_keys = pl.BlockSpec((PEER_HEADS, g, PEER_N_KEYS, LANES), lambda i, j: (0, i, 0, 0))
    slab_keys = pl.BlockSpec((PEER_HEADS, g, ik, LANES), lambda i, j: (0, i, j, 0))
    return pl.pallas_call(
        functools.partial(_peer_main_kernel, tile=tile, eblk=eblk),
        out_shape=jax.ShapeDtypeStruct((n, D_MODEL), F32),
        grid=(n // tile, PEER_N_EXPERTS // eblk),
        in_specs=[
            pl.BlockSpec((tile, D_MODEL), row),
            pl.BlockSpec((eblk, D_MODEL), lambda i, j: (j, 0)),
            pl.BlockSpec((D_MODEL, eblk), lambda i, j: (0, j)),
            full_keys, slab_keys, slab_keys,
            pl.BlockSpec((tile, D_MODEL), row),
            pl.BlockSpec((1, 6, D_MODEL), lambda i, j: (cond_fn(i), 0, 0)),
            pl.BlockSpec(lng.shape, lambda i, j: (0, 0)), pl.BlockSpec(lnb.shape, lambda i, j: (0, 0)),
        ],
        out_specs=pl.BlockSpec((tile, D_MODEL), row),
        scratch_shapes=[pltpu.VMEM((eblk, tile), F32), pltpu.VMEM((eblk, tile), MM_DTYPE),
                        pltpu.VMEM((D_MODEL, tile), F32)],
        compiler_params=_params("parallel", "arbitrary"),
        name="peer_main",
    )(h2, u, vt, e2, th, e1, x1, mod, lng, lnb)


def _peer_layer(h2, x1, mod, cond_fn, pw, lng, lnb, tile_topk, tile_main, eblk):
    e2, th, e1 = _peer_topk(h2, pw["wqt"], pw["sk"], tile_topk)
    return _peer_main(h2, pw["u"], pw["vt"], e2, th, e1, x1, mod, cond_fn, lng, lnb, tile_main, eblk)


def _pad_heads(w, n_used):
    k = w.shape[0]
    return jnp.pad(w, ((0, 0), (0, 0), (0, HEAD_PAD - n_used))).reshape(k, MLA_HEADS * HEAD_PAD)


def _rotate_half_cols(w):
    n = QK_ROPE_DIM // 4
    w4 = w.reshape(w.shape[:-1] + (2, 2, n))
    return jnp.stack([-w4[..., 1, :], w4[..., 0, :]], axis=-2).reshape(w.shape)


def _mla_weights(wdq, gq, wuq, wdkv, gkv, wukv):
    dq = QK_NOPE_DIM + QK_ROPE_DIM
    wuq3 = wuq.reshape(Q_LORA_RANK, MLA_HEADS, dq)
    pe = wuq3[..., QK_NOPE_DIM:]
    zeros_nope = jnp.zeros((Q_LORA_RANK, MLA_HEADS, QK_NOPE_DIM), F32)
    wuq_pad = _pad_heads(wuq3, dq)
    wuqs_pad = _pad_heads(jnp.concatenate([zeros_nope, _rotate_half_cols(pe)], axis=-1), dq)
    wkpe = wdkv[:, KV_LORA_RANK:]
    zk = jnp.zeros((D_MODEL, ROPE_OFF), F32)
    zt = jnp.zeros((D_MODEL, HEAD_PAD - ROPE_OFF - QK_ROPE_DIM), F32)
    wdkv_pad = jnp.concatenate([wdkv[:, :KV_LORA_RANK], zk, wkpe, zt, zk, _rotate_half_cols(wkpe), zt], axis=-1)
    wukv3 = wukv.reshape(KV_LORA_RANK, MLA_HEADS, QK_NOPE_DIM + V_HEAD_DIM)
    place = jnp.pad(jnp.eye(QK_ROPE_DIM, dtype=F32), ((0, 0), (ROPE_OFF, HEAD_PAD - ROPE_OFF - QK_ROPE_DIM)))
    return dict(
        wdq=wdq.astype(MM_DTYPE), gq=gq.reshape(1, -1), wuq=wuq_pad.astype(MM_DTYPE), wuqs=wuqs_pad.astype(MM_DTYPE),
        wdkv=wdkv_pad.astype(MM_DTYPE), gkv=gkv.reshape(1, -1),
        wukvk=_pad_heads(wukv3[..., :QK_NOPE_DIM], QK_NOPE_DIM).astype(MM_DTYPE),
        wukvv=wukv3[..., QK_NOPE_DIM:].reshape(KV_LORA_RANK, MLA_HEADS * V_HEAD_DIM).astype(MM_DTYPE),
        place=place.astype(MM_DTYPE),
    )


def _rope_tables(length):
    rows = length // GRID_W
    t = jnp.arange(rows * GRID_W)
    row = (t // GRID_W).astype(F32)
    col = (t % GRID_W).astype(F32)
    half = QK_ROPE_DIM // 2
    inv_freq = 1.0 / jnp.power(ROPE_THETA, jnp.arange(0, half, 2, dtype=F32) / half)
    n = QK_ROPE_DIM // 4
    ang = jnp.stack([row[:, None] * inv_freq, col[:, None] * inv_freq], axis=1)
    ang = jnp.broadcast_to(ang[:, :, None, :], (length, 2, 2, n)).reshape(length, QK_ROPE_DIM)
    pad = ((0, 0), (ROPE_OFF, HEAD_PAD - ROPE_OFF - QK_ROPE_DIM))
    cos = jnp.pad(jnp.cos(ang), pad, constant_values=1.0)
    sin = jnp.pad(jnp.sin(ang), pad)
    return cos, sin


def _dft_tables(length, scale):
    k = jnp.arange(length, dtype=jnp.int32)
    kt = (k[:, None] * k[None, :]) % length
    ang = kt.astype(F32) * (2.0 * math.pi / length)
    return (jnp.cos(ang) * scale).astype(MM_DTYPE), (jnp.sin(ang) * scale).astype(MM_DTYPE)


def kernel(x_prompt, x_sample, cache_ckv, cache_kpe, c, c_ctx, ada_w, ada_b, ln_mix_g, ln_mix_b, ln_ffn_g, ln_ffn_b, mla_wdq, mla_q_norm_g, mla_wuq, mla_wdkv, mla_kv_norm_g, mla_wukv, mla_wo, pool_w, pool_scale, fnet_w, fnet_b, peer_wq, peer_subkeys, peer_u, peer_v):
    nb_p, len_p, _ = x_prompt.shape
    nb_s, len_s, _ = x_sample.shape
    assert 1 + nb_s <= N_COND_ROWS
    cond = jnp.zeros((N_COND_ROWS, D_MODEL), F32).at[0].set(c_ctx).at[1:1 + nb_s].set(c)
    mods = _adaln(cond, ada_w, ada_b)

    streams = [
        dict(x=x_prompt.reshape(nb_p * len_p, D_MODEL), n_seq=nb_p, seq_len=len_p, per_seq=False),
        dict(x=x_sample.reshape(nb_s * len_s, D_MODEL), n_seq=nb_s, seq_len=len_s, per_seq=True),
    ]
    tile = 256
    tile_main = 1024
    eblk = 1024
    cos_s, sin_s = _rope_tables(len_s)
    ident_cos = jnp.ones((tile, HEAD_PAD), F32)
    ident_sin = jnp.zeros((tile, HEAD_PAD), F32)
    new_ckv, new_kpe = [], []

    for i in range(DEPTH):
        kind, j = i % N_MIXERS, i // N_MIXERS
        mod = mods[i]
        lng, lnb = ln_mix_g[i].reshape(1, -1), ln_mix_b[i].reshape(1, -1)
        pw = dict(
            wqt=peer_wq[i].T.astype(MM_DTYPE),
            sk=peer_subkeys[i].reshape(PEER_HEADS * 2, PEER_N_KEYS, PEER_HALF).astype(MM_DTYPE),
            u=peer_u[i].astype(MM_DTYPE),
            vt=peer_v[i].T.astype(MM_DTYPE),
        )
        if kind == 0:
            mw = _mla_weights(mla_wdq[j], mla_q_norm_g[j], mla_wuq[j], mla_wdkv[j], mla_kv_norm_g[j], mla_wukv[j])
            wo = mla_wo[j].astype(MM_DTYPE)
            ctx = _mla_ctx(cache_ckv, cache_kpe, j, mw)
        elif kind == 2:
            cc, sc = _dft_tables(FOURIER_GROUP_DIM, FOURIER_GROUP_DIM ** -0.5)
        for s in streams:
            x, seq_len, n_seq = s["x"], s["seq_len"], s["n_seq"]
            cond_fn = _cond_fn(s["per_seq"], tile, seq_len)
            if kind == 0:
                if s["per_seq"]:
                    tps = seq_len // tile
                    q, k, v, _, _ = _mla_proj(x, mod, cond_fn, cos_s, sin_s, lambda t, tps=tps: t % tps, mw, tile)
                    o = _attention(q, k, v, n_seq, seq_len, min(256, seq_len), ctx=ctx)
                else:
                    q, k, v, ckv, kpe = _mla_proj(x, mod, cond_fn, ident_cos, ident_sin, lambda t: 0, mw, tile)
                    o = _attention(q, k, v, n_seq, seq_len, min(256, seq_len))
                    new_ckv.append(ckv.reshape(n_seq, seq_len, KV_LORA_RANK))
                    new_kpe.append(kpe.reshape(n_seq, seq_len, QK_ROPE_DIM))
                x1, h2 = _mla_post(o, x, mod, cond_fn, wo, lng, lnb, tile)
            elif kind == 1:
                x1, h2 = _pool_layer(x, mod, cond_fn, seq_len, pool_w[j].astype(MM_DTYPE),
                                     pool_scale[j].reshape(1, -1), lng, lnb, tile)
            else:
                hc, hs = _fnet_chan(x, mod, cond_fn, cc, sc, tile)
                cl, sl = _dft_tables(seq_len, seq_len ** -0.5)
                x1, h2 = _fnet_seq(hc, hs, x, mod, s["per_seq"], n_seq, seq_len, cl, sl,
                                   fnet_w[j].astype(MM_DTYPE), fnet_b[j].reshape(1, -1), lng, lnb,
                                   min(512, seq_len), min(1024, seq_len))
            cond_main = _cond_fn(s["per_seq"], min(tile_main, x.shape[0]), seq_len)
            s["x"] = _peer_layer(h2, x1, mod, cond_main, pw, ln_ffn_g[i].reshape(1, -1), ln_ffn_b[i].reshape(1, -1),
                                 tile, min(tile_main, x.shape[0]), eblk)

    y_prompt = streams[0]["x"].reshape(nb_p, len_p, D_MODEL)
    y_sample = streams[1]["x"].reshape(nb_s, len_s, D_MODEL)
    return (y_prompt, y_sample, jnp.stack(new_ckv, axis=1), jnp.stack(new_kpe, axis=1))
```

```python
import functools
import math

import jax
import jax.numpy as jnp
from jax import lax
from jax.experimental import pallas as pl
from jax.experimental.pallas import tpu as pltpu

F32 = jnp.float32
MM_DTYPE = jnp.bfloat16

D_MODEL = 1024
DEPTH = 4
GRID_W = 64
N_MIXERS = 3

MLA_HEADS = 16
QK_NOPE_DIM = 64
QK_ROPE_DIM = 32
V_HEAD_DIM = 64
Q_LORA_RANK = 384
KV_LORA_RANK = 256
ROPE_THETA = 10000.0
HEAD_PAD = 128
ROPE_OFF = QK_NOPE_DIM

POOL_WINDOWS = (2, 4, 8, 16)
N_POOL_GROUPS = 4
POOL_GROUP_DIM = D_MODEL // N_POOL_GROUPS
POOL_HALO = 8

N_FOURIER_GROUPS = 4
FOURIER_GROUP_DIM = D_MODEL // N_FOURIER_GROUPS

PEER_HEADS = 8
PEER_N_KEYS = 128
PEER_N_EXPERTS = PEER_N_KEYS * PEER_N_KEYS
PEER_QUERY_DIM = 256
PEER_HALF = PEER_QUERY_DIM // 2
PEER_TOPK = 16

DEEPNORM_ALPHA = (2 * DEPTH) ** 0.25
LN_EPS = 1e-5
RMS_EPS = 1e-6

LANES = 128
SUBLANES = 8
N_COND_ROWS = 16
VMEM_LIMIT = 56 * 1024 * 1024

TOKEN_TILE = 256
ATTN_Q_TILE = 512
FNET_OUT_TILE = 512
FNET_IN_TILE = 1024
PEER_TOKEN_TILE = 1024
PEER_EXPERT_SLAB = 1024


def _nn(a, b):
    return jnp.dot(a, b, preferred_element_type=F32)


def _nt(a, b):
    return lax.dot_general(a, b, (((1,), (1,)), ((), ())), preferred_element_type=F32)


def _params(*sem, flags=None):
    return pltpu.CompilerParams(dimension_semantics=sem, vmem_limit_bytes=VMEM_LIMIT, flags=flags)


def _layer_norm(z, g, b):
    mu = jnp.mean(z, axis=-1, keepdims=True)
    d = z - mu
    var = jnp.mean(d * d, axis=-1, keepdims=True)
    return d * lax.rsqrt(var + LN_EPS) * g + b


def _rms_norm(x, g):
    return x * lax.rsqrt(jnp.mean(x * x, axis=-1, keepdims=True) + RMS_EPS) * g


def _full(shape):
    n = len(shape)
    return pl.BlockSpec(shape, lambda *_: (0,) * n)


def _mod_spec(cond_fn):
    return pl.BlockSpec((1, 6, D_MODEL), lambda i, *_: (cond_fn(i), 0, 0))


def _cond_fn(per_seq, tile, seq_len):
    if not per_seq:
        return lambda i: 0
    return lambda i: 1 + (i * tile) // seq_len


def _adaln_kernel(cond_ref, w_ref, b_ref, o_ref):
    c = cond_ref[...]
    c = c / (1.0 + jnp.exp(-c))
    o_ref[0] = jnp.dot(c, w_ref[0], preferred_element_type=F32,
                       precision=lax.Precision.HIGHEST) + b_ref[0]


def _adaln(cond, ada_w, ada_b):
    nb = 1536
    out = pl.pallas_call(
        _adaln_kernel,
        out_shape=jax.ShapeDtypeStruct((DEPTH, N_COND_ROWS, 6 * D_MODEL), F32),
        grid=(DEPTH, 6 * D_MODEL // nb),
        in_specs=[
            pl.BlockSpec((N_COND_ROWS, D_MODEL), lambda i, j: (0, 0)),
            pl.BlockSpec((1, D_MODEL, nb), lambda i, j: (i, 0, j)),
            pl.BlockSpec((1, 1, nb), lambda i, j: (i, 0, j)),
        ],
        out_specs=pl.BlockSpec((1, N_COND_ROWS, nb), lambda i, j: (i, 0, j)),
        compiler_params=_params("parallel", "parallel"),
        name="adaln",
    )(cond, ada_w, ada_b.reshape(DEPTH, 1, 6 * D_MODEL))
    return out.reshape(DEPTH, N_COND_ROWS, 6, D_MODEL)


def _mla_proj_kernel(x_ref, mod_ref, cos_ref, sin_ref, wdq_ref, gq_ref, wuq_ref, wuqs_ref,
                     wdkv_ref, gkv_ref, wukvk_ref, wukvv_ref,
                     q_ref, k_ref, v_ref, ckv_ref, kpe_ref):
    h = x_ref[...] * (1.0 + mod_ref[0, 1:2, :]) + mod_ref[0, 0:1, :]
    hb = h.astype(MM_DTYPE)
    cq = _rms_norm(_nn(hb, wdq_ref[...]), gq_ref[...]).astype(MM_DTYPE)
    q = _nn(cq, wuq_ref[...])
    qs = _nn(cq, wuqs_ref[...])
    kv = _nn(hb, wdkv_ref[...])
    ckv = _rms_norm(kv[:, :KV_LORA_RANK], gkv_ref[...])
    kpe = kv[:, KV_LORA_RANK:KV_LORA_RANK + HEAD_PAD]
    kpes = kv[:, KV_LORA_RANK + HEAD_PAD:]
    cos = cos_ref[...]
    sin = sin_ref[...]
    kpe_rot = kpe * cos + kpes * sin
    ckvb = ckv.astype(MM_DTYPE)
    kn = _nn(ckvb, wukvk_ref[...])
    scale = (QK_NOPE_DIM + QK_ROPE_DIM) ** -0.5 * math.log2(math.e)
    for hd in range(MLA_HEADS):
        sl = slice(hd * HEAD_PAD, (hd + 1) * HEAD_PAD)
        q_ref[:, sl] = ((q[:, sl] * cos + qs[:, sl] * sin) * scale).astype(q_ref.dtype)
        k_ref[:, sl] = (kn[:, sl] + kpe_rot).astype(k_ref.dtype)
    v_ref[0], v_ref[1] = _split_values(_nn(ckvb, wukvv_ref[...]), v_ref.dtype)
    ckv_ref[...] = ckv
    kpe_ref[...] = kpe[:, ROPE_OFF:ROPE_OFF + QK_ROPE_DIM]


def _mla_proj(x, mod, cond_fn, cos, sin, pos_fn, w, tile):
    n = x.shape[0]
    hp = MLA_HEADS * HEAD_PAD
    hv = MLA_HEADS * V_HEAD_DIM
    row = lambda i: (i, 0)
    return pl.pallas_call(
        _mla_proj_kernel,
        out_shape=(
            jax.ShapeDtypeStruct((n, hp), MM_DTYPE),
            jax.ShapeDtypeStruct((n, hp), MM_DTYPE),
            jax.ShapeDtypeStruct((2, n, hv), MM_DTYPE),
            jax.ShapeDtypeStruct((n, KV_LORA_RANK), F32),
            jax.ShapeDtypeStruct((n, QK_ROPE_DIM), F32),
        ),
        grid=(n // tile,),
        in_specs=[
            pl.BlockSpec((tile, D_MODEL), row),
            _mod_spec(cond_fn),
            pl.BlockSpec((tile, HEAD_PAD), lambda i: (pos_fn(i), 0)),
            pl.BlockSpec((tile, HEAD_PAD), lambda i: (pos_fn(i), 0)),
            _full(w["wdq"].shape), _full(w["gq"].shape), _full(w["wuq"].shape), _full(w["wuqs"].shape),
            _full(w["wdkv"].shape), _full(w["gkv"].shape), _full(w["wukvk"].shape), _full(w["wukvv"].shape),
        ],
        out_specs=(
            pl.BlockSpec((tile, hp), row),
            pl.BlockSpec((tile, hp), row),
            pl.BlockSpec((2, tile, hv), lambda i: (0, i, 0)),
            pl.BlockSpec((tile, KV_LORA_RANK), row),
            pl.BlockSpec((tile, QK_ROPE_DIM), row),
        ),
        compiler_params=_params("parallel"),
        name="mla_proj",
    )(x, mod, cos, sin, w["wdq"], w["gq"], w["wuq"], w["wuqs"], w["wdkv"], w["gkv"], w["wukvk"], w["wukvv"])


def _mla_ctx_kernel(ckv_ref, kpe_ref, wukvk_ref, wukvv_ref, place_ref, k_ref, v_ref):
    ckvb = ckv_ref[...].astype(MM_DTYPE)
    kn = _nn(ckvb, wukvk_ref[...])
    kpe = _nn(kpe_ref[...].astype(MM_DTYPE), place_ref[...])
    for hd in range(MLA_HEADS):
        sl = slice(hd * HEAD_PAD, (hd + 1) * HEAD_PAD)
        k_ref[:, sl] = (kn[:, sl] + kpe).astype(k_ref.dtype)
    v_ref[0], v_ref[1] = _split_values(_nn(ckvb, wukvv_ref[...]), v_ref.dtype)


def _mla_ctx(cache_ckv, cache_kpe, j, w):
    nb, _, past, _ = cache_ckv.shape
    hp = MLA_HEADS * HEAD_PAD
    hv = MLA_HEADS * V_HEAD_DIM
    return pl.pallas_call(
        _mla_ctx_kernel,
        out_shape=(jax.ShapeDtypeStruct((nb * past, hp), MM_DTYPE),
                   jax.ShapeDtypeStruct((2, nb * past, hv), MM_DTYPE)),
        grid=(nb,),
        in_specs=[
            pl.BlockSpec((None, None, past, KV_LORA_RANK), lambda b: (b, j, 0, 0)),
            pl.BlockSpec((None, None, past, QK_ROPE_DIM), lambda b: (b, j, 0, 0)),
            _full(w["wukvk"].shape), _full(w["wukvv"].shape), _full(w["place"].shape),
        ],
        out_specs=(pl.BlockSpec((past, hp), lambda b: (b, 0)),
                   pl.BlockSpec((2, past, hv), lambda b: (0, b, 0))),
        compiler_params=_params("parallel"),
        name="mla_ctx",
    )(cache_ckv, cache_kpe, w["wukvk"], w["wukvv"], w["place"])


def _attn_kernel(*refs, has_ctx):
    if has_ctx:
        q_ref, k_ref, v_ref, kc_ref, vc_ref, o_ref = refs
    else:
        q_ref, k_ref, v_ref, o_ref = refs
    scores = []
    for hh in range(2):
        sl = slice(hh * HEAD_PAD, (hh + 1) * HEAD_PAD)
        q = q_ref[:, sl]
        s = _nt(q, k_ref[:, sl])
        m = jnp.max(s, axis=-1, keepdims=True)
        sc = None
        if has_ctx:
            sc = _nt(q, kc_ref[:, sl])
            m = jnp.maximum(m, jnp.max(sc, axis=-1, keepdims=True))
        scores.append((s, sc, m))
    probs = []
    for s, sc, m in scores:
        p = jnp.exp2(s - m).astype(MM_DTYPE)
        pc = jnp.exp2(sc - m).astype(MM_DTYPE) if has_ctx else None
        probs.append((p, pc))
    outs = []
    for hh, (p, pc) in enumerate(probs):
        o = _nn(p, v_ref[hh])
        if has_ctx:
            o = o + _nn(pc, vc_ref[hh])
        ol = _ones_lane(hh)
        outs.append(o * (1.0 / o[:, ol:ol + 1]))
    lane = lax.broadcasted_iota(jnp.int32, (1, 2 * V_HEAD_DIM), 1)
    o_ref[...] = jnp.where(lane < V_HEAD_DIM, outs[0], outs[1]).astype(o_ref.dtype)


def _ones_lane(hh):
    return V_HEAD_DIM if hh == 0 else 0


def _split_values(v, dtype):
    lane = lax.broadcasted_iota(jnp.int32, (1, 2 * V_HEAD_DIM), 1)
    parts = []
    for hh in range(2):
        own = (lane < V_HEAD_DIM) if hh == 0 else (lane >= V_HEAD_DIM)
        fill = jnp.where(lane == _ones_lane(hh), 1.0, 0.0)
        blocks = [jnp.where(own, v[:, p * 2 * V_HEAD_DIM:(p + 1) * 2 * V_HEAD_DIM], fill)
                  for p in range(MLA_HEADS // 2)]
        parts.append(jnp.concatenate(blocks, axis=-1).astype(dtype))
    return parts


def _attention(q, k, v, n_seq, seq_len, tq, ctx=None):
    n = q.shape[0]
    hpairs = MLA_HEADS // 2
    qb = seq_len // tq
    in_specs = [
        pl.BlockSpec((tq, 2 * HEAD_PAD), lambda b, h, i: (b * qb + i, h)),
        pl.BlockSpec((seq_len, 2 * HEAD_PAD), lambda b, h, i: (b, h)),
        pl.BlockSpec((2, seq_len, 2 * V_HEAD_DIM), lambda b, h, i: (0, b, h)),
    ]
    args = [q, k, v]
    if ctx is not None:
        kc, vc = ctx
        past = kc.shape[0] // n_seq
        in_specs += [
            pl.BlockSpec((past, 2 * HEAD_PAD), lambda b, h, i: (b, h)),
            pl.BlockSpec((2, past, 2 * V_HEAD_DIM), lambda b, h, i: (0, b, h)),
        ]
        args += [kc, vc]
    return pl.pallas_call(
        functools.partial(_attn_kernel, has_ctx=ctx is not None),
        out_shape=jax.ShapeDtypeStruct((n, MLA_HEADS * V_HEAD_DIM), MM_DTYPE),
        grid=(n_seq, hpairs, qb),
        in_specs=in_specs,
        out_specs=pl.BlockSpec((tq, 2 * V_HEAD_DIM), lambda b, h, i: (b * qb + i, h)),
        compiler_params=_params("parallel", "parallel", "arbitrary"),
        name="mla_attn",
    )(*args)


def _post_store(z, mod_ref, lng_ref, lnb_ref, x1_ref, h2_ref):
    x1 = _layer_norm(z, lng_ref[...], lnb_ref[...])
    x1_ref[...] = x1
    h2_ref[...] = (x1 * (1.0 + mod_ref[0, 4:5, :]) + mod_ref[0, 3:4, :]).astype(h2_ref.dtype)


def _mla_post_kernel(o_ref, x_ref, mod_ref, wo_ref, lng_ref, lnb_ref, x1_ref, h2_ref):
    y = _nn(o_ref[...], wo_ref[...])
    z = DEEPNORM_ALPHA * x_ref[...] + mod_ref[0, 2:3, :] * y
    _post_store(z, mod_ref, lng_ref, lnb_ref, x1_ref, h2_ref)


def _mla_post(o, x, mod, cond_fn, wo, lng, lnb, tile):
    n = x.shape[0]
    row = lambda i: (i, 0)
    return pl.pallas_call(
        _mla_post_kernel,
        out_shape=(jax.ShapeDtypeStruct((n, D_MODEL), F32), jax.ShapeDtypeStruct((n, D_MODEL), MM_DTYPE)),
        grid=(n // tile,),
        in_specs=[
            pl.BlockSpec((tile, D_MODEL), row), pl.BlockSpec((tile, D_MODEL), row), _mod_spec(cond_fn),
            _full(wo.shape), _full(lng.shape), _full(lnb.shape),
        ],
        out_specs=(pl.BlockSpec((tile, D_MODEL), row), pl.BlockSpec((tile, D_MODEL), row)),
        compiler_params=_params("parallel"),
        name="mla_post",
    )(o, x, mod, wo, lng, lnb)


def _pool_kernel(x_ref, xp_ref, xn_ref, mod_ref, w_ref, ps_ref, lng_ref, lnb_ref, x1_ref, h2_ref, ext_ref,
                 *, tile, tiles_per_seq):
    i = pl.program_id(0)
    pos = i % tiles_per_seq
    scale1 = 1.0 + mod_ref[0, 1:2, :]
    shift1 = mod_ref[0, 0:1, :]
    x = x_ref[...]
    keep_prev = (pos > 0).astype(F32)
    keep_next = (pos < tiles_per_seq - 1).astype(F32)
    ext_ref[0:POOL_HALO, :] = (xp_ref[...] * scale1 + shift1) * keep_prev
    ext_ref[POOL_HALO:POOL_HALO + tile, :] = x * scale1 + shift1
    ext_ref[POOL_HALO + tile:, :] = (xn_ref[...] * scale1 + shift1) * keep_next
    seq_len = tile * tiles_per_seq
    t = pos * tile + lax.broadcasted_iota(jnp.int32, (tile, POOL_GROUP_DIM), 0)
    ys = []
    for g, w in enumerate(POOL_WINDOWS):
        cs = slice(g * POOL_GROUP_DIM, (g + 1) * POOL_GROUP_DIM)
        tot = None
        for d in range(-(w // 2), w - w // 2):
            piece = ext_ref[POOL_HALO + d:POOL_HALO + d + tile, cs]
            tot = piece if tot is None else tot + piece
        lo = jnp.maximum(t - w // 2, 0)
        hi = jnp.minimum(t + (w - w // 2) - 1, seq_len - 1)
        cnt = (hi - lo + 1).astype(F32)
        diff = tot / cnt - ext_ref[POOL_HALO:POOL_HALO + tile, cs]
        ys.append(_nn(diff.astype(MM_DTYPE), w_ref[g]))
    y = jnp.concatenate(ys, axis=-1) * ps_ref[...]
    z = DEEPNORM_ALPHA * x + mod_ref[0, 2:3, :] * y
    _post_store(z, mod_ref, lng_ref, lnb_ref, x1_ref, h2_ref)


def _pool_layer(x, mod, cond_fn, seq_len, w, ps, lng, lnb, tile):
    n = x.shape[0]
    tps = seq_len // tile
    hb = tile // POOL_HALO
    nblk = n // POOL_HALO
    row = lambda i: (i, 0)
    return pl.pallas_call(
        functools.partial(_pool_kernel, tile=tile, tiles_per_seq=tps),
        out_shape=(jax.ShapeDtypeStruct((n, D_MODEL), F32), jax.ShapeDtypeStruct((n, D_MODEL), MM_DTYPE)),
        grid=(n // tile,),
        in_specs=[
            pl.BlockSpec((tile, D_MODEL), row),
            pl.BlockSpec((POOL_HALO, D_MODEL), lambda i: (jnp.maximum(i * hb - 1, 0), 0)),
            pl.BlockSpec((POOL_HALO, D_MODEL), lambda i: (jnp.minimum((i + 1) * hb, nblk - 1), 0)),
            _mod_spec(cond_fn), _full(w.shape), _full(ps.shape), _full(lng.shape), _full(lnb.shape),
        ],
        out_specs=(pl.BlockSpec((tile, D_MODEL), row), pl.BlockSpec((tile, D_MODEL), row)),
        scratch_shapes=[pltpu.VMEM((tile + 2 * POOL_HALO, D_MODEL), F32)],
        compiler_params=_params("parallel"),
        name="pool_layer",
    )(x, x, x, mod, w, ps, lng, lnb)


def _fnet_chan_kernel(x_ref, mod_ref, cc_ref, sc_ref, hc_ref, hs_ref):
    h = x_ref[...] * (1.0 + mod_ref[0, 1:2, :]) + mod_ref[0, 0:1, :]
    hb = h.astype(MM_DTYPE)
    for g in range(N_FOURIER_GROUPS):
        cs = slice(g * FOURIER_GROUP_DIM, (g + 1) * FOURIER_GROUP_DIM)
        hc_ref[:, cs] = _nn(hb[:, cs], cc_ref[...]).astype(hc_ref.dtype)
        hs_ref[:, cs] = _nn(hb[:, cs], sc_ref[...]).astype(hs_ref.dtype)


def _fnet_chan(x, mod, cond_fn, cc, sc, tile):
    n = x.shape[0]
    row = lambda i: (i, 0)
    return pl.pallas_call(
        _fnet_chan_kernel,
        out_shape=(jax.ShapeDtypeStruct((n, D_MODEL), MM_DTYPE), jax.ShapeDtypeStruct((n, D_MODEL), MM_DTYPE)),
        grid=(n // tile,),
        in_specs=[pl.BlockSpec((tile, D_MODEL), row), _mod_spec(cond_fn), _full(cc.shape), _full(sc.shape)],
        out_specs=(pl.BlockSpec((tile, D_MODEL), row), pl.BlockSpec((tile, D_MODEL), row)),
        compiler_params=_params("parallel"),
        name="fnet_chan",
    )(x, mod, cc, sc)


def _fnet_seq_kernel(cl_ref, sl_ref, hc_ref, hs_ref, x_ref, mod_ref, fw_ref, fb_ref, lng_ref, lnb_ref,
                     x1_ref, h2_ref, acc_ref):
    tt = pl.program_id(2)

    @pl.when(tt == 0)
    def _():
        acc_ref[...] = jnp.zeros_like(acc_ref)

    acc_ref[...] += _nn(cl_ref[...], hc_ref[...]) - _nn(sl_ref[...], hs_ref[...])

    @pl.when(tt == pl.num_programs(2) - 1)
    def _():
        y = _nn(acc_ref[...].astype(MM_DTYPE), fw_ref[...]) + fb_ref[...]
        z = DEEPNORM_ALPHA * x_ref[...] + mod_ref[0, 2:3, :] * y
        _post_store(z, mod_ref, lng_ref, lnb_ref, x1_ref, h2_ref)


def _fnet_seq(hc, hs, x, mod, per_seq, n_seq, seq_len, cl, sl, fw, fb, lng, lnb, tk, tt):
    n = x.shape[0]
    kb = seq_len // tk
    tb = seq_len // tt
    out_row = lambda b, k, t: (b * kb + k, 0)
    cond = (lambda b, k, t: (1 + b, 0, 0)) if per_seq else (lambda b, k, t: (0, 0, 0))
    return pl.pallas_call(
        _fnet_seq_kernel,
        out_shape=(jax.ShapeDtypeStruct((n, D_MODEL), F32), jax.ShapeDtypeStruct((n, D_MODEL), MM_DTYPE)),
        grid=(n_seq, kb, tb),
        in_specs=[
            pl.BlockSpec((tk, tt), lambda b, k, t: (k, t)),
            pl.BlockSpec((tk, tt), lambda b, k, t: (k, t)),
            pl.BlockSpec((tt, D_MODEL), lambda b, k, t: (b * tb + t, 0)),
            pl.BlockSpec((tt, D_MODEL), lambda b, k, t: (b * tb + t, 0)),
            pl.BlockSpec((tk, D_MODEL), out_row),
            pl.BlockSpec((1, 6, D_MODEL), cond),
            _full(fw.shape), _full(fb.shape), _full(lng.shape), _full(lnb.shape),
        ],
        out_specs=(pl.BlockSpec((tk, D_MODEL), out_row), pl.BlockSpec((tk, D_MODEL), out_row)),
        scratch_shapes=[pltpu.VMEM((tk, D_MODEL), F32)],
        compiler_params=_params("parallel", "parallel", "arbitrary"),
        name="fnet_seq",
    )(cl, sl, hc, hs, x, mod, fw, fb, lng, lnb)


def _top16(x):
    v = _sort_desc([x[SUBLANES * r:SUBLANES * (r + 1)] for r in range(x.shape[0] // SUBLANES)])
    rows = []
    for k in range(PEER_TOPK):
        m = jnp.max(v[0], axis=0, keepdims=True)
        rows.append(m)
        hit = v[0] == m
        v = [jnp.where(hit, v[d + 1], v[d]) for d in range(PEER_TOPK - 1 - k)]
    return rows


def _sort_desc(v):
    v = list(v)
    n = len(v)
    p = 1
    while p < n:
        k = p
        while k >= 1:
            for j in range(k % p, n - k, 2 * k):
                for i in range(min(k, n - j - k)):
                    if (i + j) // (2 * p) == (i + j + k) // (2 * p):
                        a, b = v[i + j], v[i + j + k]
                        v[i + j], v[i + j + k] = jnp.maximum(a, b), jnp.minimum(a, b)
            k //= 2
        p *= 2
    return v


def _stack8(rows, row8):
    out = jnp.broadcast_to(rows[0], row8.shape)
    for k in range(1, 8):
        out = jnp.where(row8 == k, rows[k], out)
    return out


def _peer_topk_kernel(h_ref, wqt_ref, sk_ref, e2_ref, th_ref, e1_ref, qt_ref, *, tile):
    qt_ref[...] = _nt(wqt_ref[...], h_ref[...]).astype(qt_ref.dtype)
    row8 = lax.broadcasted_iota(jnp.int32, (8, LANES), 0)

    def head(hd, carry):
        base = pl.multiple_of(hd * PEER_QUERY_DIM, PEER_QUERY_DIM)
        for lg in range(tile // LANES):
            ls = slice(lg * LANES, (lg + 1) * LANES)
            s1 = _nn(sk_ref[2 * hd], qt_ref[pl.ds(base, PEER_HALF), ls])
            s2 = _nn(sk_ref[2 * hd + 1], qt_ref[pl.ds(base + PEER_HALF, PEER_HALF), ls])
            t1 = _top16(s1)
            t2 = _top16(s2)
            t1a = _stack8(t1[:8], row8)
            lists = [t1a + t2[0]] + [jnp.where(row8 < PEER_TOPK // (l + 1), t1a + t2[l], -jnp.inf)
                                     for l in range(1, PEER_TOPK)]
            tail = _stack8(t1[8:], row8) + t2[0]
            top = t1[0] + t2[0]
            z = jnp.zeros_like(top)
            tau = top
            for j in range(PEER_TOPK):
                tau = jnp.max(jnp.maximum(lists[0], tail), axis=0, keepdims=True)
                z = z + jnp.exp(tau - top)
                if j < PEER_TOPK - 1:
                    hit = lists[0] == tau
                    tail = jnp.where(tail == tau, -jnp.inf, tail)
                    lists = [jnp.where(hit, lists[d + 1], lists[d]) for d in range(len(lists) - 1)]
            e2_of = lambda s: jnp.exp(s - t2[0]) * (0.5 / z)
            theta = jnp.full_like(s1, jnp.inf)
            for l in range(PEER_TOPK // 2):
                theta = jnp.where(s1 + t2[l] >= tau, e2_of(t2[l]), theta)
            best = jnp.full_like(top, jnp.inf)
            for l in range(PEER_TOPK // 2, PEER_TOPK):
                best = jnp.where(t1[0] + t2[l] >= tau, e2_of(t2[l]), best)
            theta = jnp.where(s1 == t1[0], jnp.minimum(theta, best), theta)
            e2_ref[hd, lg] = e2_of(s2)
            th_ref[hd, lg] = theta
            e1_ref[hd, lg] = jnp.exp(s1 - t1[0])
        return carry

    lax.fori_loop(0, PEER_HEADS, head, 0)


def _peer_topk(h2, wqt, sk, tile):
    n = h2.shape[0]
    g = tile // LANES
    shp = jax.ShapeDtypeStruct((PEER_HEADS, n // LANES, PEER_N_KEYS, LANES), F32)
    spec = pl.BlockSpec((PEER_HEADS, g, PEER_N_KEYS, LANES), lambda i: (0, i, 0, 0))
    return pl.pallas_call(
        functools.partial(_peer_topk_kernel, tile=tile),
        out_shape=(shp, shp, shp),
        grid=(n // tile,),
        in_specs=[pl.BlockSpec((tile, D_MODEL), lambda i: (i, 0)), _full(wqt.shape), _full(sk.shape)],
        out_specs=(spec, spec, spec),
        scratch_shapes=[pltpu.VMEM((PEER_HEADS * PEER_QUERY_DIM, tile), MM_DTYPE)],
        compiler_params=_params("parallel"),
        name="peer_topk",
    )(h2, wqt, sk)


def _peer_main_kernel(h_ref, u_ref, vt_ref, e2_ref, th_ref, e1_ref, x1_ref, mod_ref, lng_ref, lnb_ref,
                      out_ref, at_ref, hm_ref, acc_ref, *, tile, eblk):
    j = pl.program_id(1)

    @pl.when(j == 0)
    def _():
        acc_ref[...] = jnp.zeros_like(acc_ref)

    at_ref[...] = _nt(u_ref[...], h_ref[...])
    for lg in range(tile // LANES):
        ls = slice(lg * LANES, (lg + 1) * LANES)
        for il in range(eblk // PEER_N_KEYS):
            rs = slice(il * PEER_N_KEYS, (il + 1) * PEER_N_KEYS)
            gate = None
            for hd in range(PEER_HEADS):
                e2 = e2_ref[hd, lg]
                w = jnp.where(e2 >= th_ref[hd, lg, il:il + 1, :], e1_ref[hd, lg, il:il + 1, :] * e2, 0.0)
                gate = w if gate is None else gate + w
            a = at_ref[rs, ls]
            hm_ref[rs, ls] = (gate * a * (1.0 + lax.erf(a * (2.0 ** -0.5)))).astype(hm_ref.dtype)
    acc_ref[...] += _nn(vt_ref[...], hm_ref[...])

    @pl.when(j == pl.num_programs(1) - 1)
    def _():
        f = acc_ref[...].T
        z = DEEPNORM_ALPHA * x1_ref[...] + mod_ref[0, 5:6, :] * f
        out_ref[...] = _layer_norm(z, lng_ref[...], lnb_ref[...])


def _peer_main(h2, u, vt, e2, th, e1, x1, mod, cond_fn, lng, lnb, tile, eblk):
    n = h2.shape[0]
    g = tile // LANES
    ik = eblk // PEER_N_KEYS
    row = lambda i, j: (i, 0)
    full_keys = pl.BlockSpec((PEER_HEADS, g, PEER_N_KEYS, LANES), lambda i, j: (0, i, 0, 0))
    slab_keys = pl.BlockSpec((PEER_HEADS, g, ik, LANES), lambda i, j: (0, i, j, 0))
    return pl.pallas_call(
        functools.partial(_peer_main_kernel, tile=tile, eblk=eblk),
        out_shape=jax.ShapeDtypeStruct((n, D_MODEL), F32),
        grid=(n // tile, PEER_N_EXPERTS // eblk),
        in_specs=[
            pl.BlockSpec((tile, D_MODEL), row),
            pl.BlockSpec((eblk, D_MODEL), lambda i, j: (j, 0)),
            pl.BlockSpec((D_MODEL, eblk), lambda i, j: (0, j)),
            full_keys, slab_keys, slab_keys,
            pl.BlockSpec((tile, D_MODEL), row),
            pl.BlockSpec((1, 6, D_MODEL), lambda i, j: (cond_fn(i), 0, 0)),
            pl.BlockSpec(lng.shape, lambda i, j: (0, 0)), pl.BlockSpec(lnb.shape, lambda i, j: (0, 0)),
        ],
        out_specs=pl.BlockSpec((tile, D_MODEL), row),
        scratch_shapes=[pltpu.VMEM((eblk, tile), F32), pltpu.VMEM((eblk, tile), MM_DTYPE),
                        pltpu.VMEM((D_MODEL, tile), F32)],
        compiler_params=_params("parallel", "arbitrary"),
        name="peer_main",
    )(h2, u, vt, e2, th, e1, x1, mod, lng, lnb)


def _peer_layer(h2, x1, mod, cond_fn, pw, lng, lnb, tile_topk, tile_main, eblk):
    e2, th, e1 = _peer_topk(h2, pw["wqt"], pw["sk"], tile_topk)
    return _peer_main(h2, pw["u"], pw["vt"], e2, th, e1, x1, mod, cond_fn, lng, lnb, tile_main, eblk)


def _pad_heads(w, n_used):
    k = w.shape[0]
    return jnp.pad(w, ((0, 0), (0, 0), (0, HEAD_PAD - n_used))).reshape(k, MLA_HEADS * HEAD_PAD)


def _rotate_half_cols(w):
    n = QK_ROPE_DIM // 4
    w4 = w.reshape(w.shape[:-1] + (2, 2, n))
    return jnp.stack([-w4[..., 1, :], w4[..., 0, :]], axis=-2).reshape(w.shape)


def _mla_weights(wdq, gq, wuq, wdkv, gkv, wukv):
    dq = QK_NOPE_DIM + QK_ROPE_DIM
    wuq3 = wuq.reshape(Q_LORA_RANK, MLA_HEADS, dq)
    pe = wuq3[..., QK_NOPE_DIM:]
    zeros_nope = jnp.zeros((Q_LORA_RANK, MLA_HEADS, QK_NOPE_DIM), F32)
    wuq_pad = _pad_heads(wuq3, dq)
    wuqs_pad = _pad_heads(jnp.concatenate([zeros_nope, _rotate_half_cols(pe)], axis=-1), dq)
    wkpe = wdkv[:, KV_LORA_RANK:]
    zk = jnp.zeros((D_MODEL, ROPE_OFF), F32)
    zt = jnp.zeros((D_MODEL, HEAD_PAD - ROPE_OFF - QK_ROPE_DIM), F32)
    wdkv_pad = jnp.concatenate([wdkv[:, :KV_LORA_RANK], zk, wkpe, zt, zk, _rotate_half_cols(wkpe), zt], axis=-1)
    wukv3 = wukv.reshape(KV_LORA_RANK, MLA_HEADS, QK_NOPE_DIM + V_HEAD_DIM)
    place = jnp.pad(jnp.eye(QK_ROPE_DIM, dtype=F32), ((0, 0), (ROPE_OFF, HEAD_PAD - ROPE_OFF - QK_ROPE_DIM)))
    return dict(
        wdq=wdq.astype(MM_DTYPE), gq=gq.reshape(1, -1), wuq=wuq_pad.astype(MM_DTYPE), wuqs=wuqs_pad.astype(MM_DTYPE),
        wdkv=wdkv_pad.astype(MM_DTYPE), gkv=gkv.reshape(1, -1),
        wukvk=_pad_heads(wukv3[..., :QK_NOPE_DIM], QK_NOPE_DIM).astype(MM_DTYPE),
        wukvv=wukv3[..., QK_NOPE_DIM:].reshape(KV_LORA_RANK, MLA_HEADS * V_HEAD_DIM).astype(MM_DTYPE),
        place=place.astype(MM_DTYPE),
    )


def _rope_tables(length):
    rows = length // GRID_W
    t = jnp.arange(rows * GRID_W)
    row = (t // GRID_W).astype(F32)
    col = (t % GRID_W).astype(F32)
    half = QK_ROPE_DIM // 2
    inv_freq = 1.0 / jnp.power(ROPE_THETA, jnp.arange(0, half, 2, dtype=F32) / half)
    n = QK_ROPE_DIM // 4
    ang = jnp.stack([row[:, None] * inv_freq, col[:, None] * inv_freq], axis=1)
    ang = jnp.broadcast_to(ang[:, :, None, :], (length, 2, 2, n)).reshape(length, QK_ROPE_DIM)
    pad = ((0, 0), (ROPE_OFF, HEAD_PAD - ROPE_OFF - QK_ROPE_DIM))
    cos = jnp.pad(jnp.cos(ang), pad, constant_values=1.0)
    sin = jnp.pad(jnp.sin(ang), pad)
    return cos, sin


def _dft_tables(length, scale):
    k = jnp.arange(length, dtype=jnp.int32)
    kt = (k[:, None] * k[None, :]) % length
    ang = kt.astype(F32) * (2.0 * math.pi / length)
    return (jnp.cos(ang) * scale).astype(MM_DTYPE), (jnp.sin(ang) * scale).astype(MM_DTYPE)


def kernel(x_prompt, x_sample, cache_ckv, cache_kpe, c, c_ctx, ada_w, ada_b, ln_mix_g, ln_mix_b, ln_ffn_g, ln_ffn_b, mla_wdq, mla_q_norm_g, mla_wuq, mla_wdkv, mla_kv_norm_g, mla_wukv, mla_wo, pool_w, pool_scale, fnet_w, fnet_b, peer_wq, peer_subkeys, peer_u, peer_v):
    nb_p, len_p, _ = x_prompt.shape
    nb_s, len_s, _ = x_sample.shape
    assert 1 + nb_s <= N_COND_ROWS
    cond = jnp.zeros((N_COND_ROWS, D_MODEL), F32).at[0].set(c_ctx).at[1:1 + nb_s].set(c)
    mods = _adaln(cond, ada_w, ada_b)

    streams = [
        dict(x=x_prompt.reshape(nb_p * len_p, D_MODEL), n_seq=nb_p, seq_len=len_p, per_seq=False),
        dict(x=x_sample.reshape(nb_s * len_s, D_MODEL), n_seq=nb_s, seq_len=len_s, per_seq=True),
    ]
    tile = TOKEN_TILE
    cos_s, sin_s = _rope_tables(len_s)
    ident_cos = jnp.ones((tile, HEAD_PAD), F32)
    ident_sin = jnp.zeros((tile, HEAD_PAD), F32)
    new_ckv, new_kpe = [], []

    for i in range(DEPTH):
        kind, j = i % N_MIXERS, i // N_MIXERS
        mod = mods[i]
        lng, lnb = ln_mix_g[i].reshape(1, -1), ln_mix_b[i].reshape(1, -1)
        pw = dict(
            wqt=peer_wq[i].T.astype(MM_DTYPE),
            sk=peer_subkeys[i].reshape(PEER_HEADS * 2, PEER_N_KEYS, PEER_HALF).astype(MM_DTYPE),
            u=peer_u[i].astype(MM_DTYPE),
            vt=peer_v[i].T.astype(MM_DTYPE),
        )
        if kind == 0:
            mw = _mla_weights(mla_wdq[j], mla_q_norm_g[j], mla_wuq[j], mla_wdkv[j], mla_kv_norm_g[j], mla_wukv[j])
            wo = mla_wo[j].astype(MM_DTYPE)
            ctx = _mla_ctx(cache_ckv, cache_kpe, j, mw)
        elif kind == 2:
            cc, sc = _dft_tables(FOURIER_GROUP_DIM, FOURIER_GROUP_DIM ** -0.5)
        for s in streams:
            x, seq_len, n_seq = s["x"], s["seq_len"], s["n_seq"]
            cond_fn = _cond_fn(s["per_seq"], tile, seq_len)
            if kind == 0:
                if s["per_seq"]:
                    tps = seq_len // tile
                    q, k, v, _, _ = _mla_proj(x, mod, cond_fn, cos_s, sin_s, lambda t, tps=tps: t % tps, mw, tile)
                    o = _attention(q, k, v, n_seq, seq_len, min(ATTN_Q_TILE, seq_len), ctx=ctx)
                else:
                    q, k, v, ckv, kpe = _mla_proj(x, mod, cond_fn, ident_cos, ident_sin, lambda t: 0, mw, tile)
                    o = _attention(q, k, v, n_seq, seq_len, min(ATTN_Q_TILE, seq_len))
                    new_ckv.append(ckv.reshape(n_seq, seq_len, KV_LORA_RANK))
                    new_kpe.append(kpe.reshape(n_seq, seq_len, QK_ROPE_DIM))
                x1, h2 = _mla_post(o, x, mod, cond_fn, wo, lng, lnb, tile)
            elif kind == 1:
                x1, h2 = _pool_layer(x, mod, cond_fn, seq_len, pool_w[j].astype(MM_DTYPE),
                                     pool_scale[j].reshape(1, -1), lng, lnb, tile)
            else:
                hc, hs = _fnet_chan(x, mod, cond_fn, cc, sc, tile)
                cl, sl = _dft_tables(seq_len, seq_len ** -0.5)
                x1, h2 = _fnet_seq(hc, hs, x, mod, s["per_seq"], n_seq, seq_len, cl, sl,
                                   fnet_w[j].astype(MM_DTYPE), fnet_b[j].reshape(1, -1), lng, lnb,
                                   min(FNET_OUT_TILE, seq_len), min(FNET_IN_TILE, seq_len))
            tile_main = min(PEER_TOKEN_TILE, x.shape[0])
            cond_main = _cond_fn(s["per_seq"], tile_main, seq_len)
            s["x"] = _peer_layer(h2, x1, mod, cond_main, pw, ln_ffn_g[i].reshape(1, -1), ln_ffn_b[i].reshape(1, -1),
                                 tile, tile_main, PEER_EXPERT_SLAB)

    y_prompt = streams[0]["x"].reshape(nb_p, len_p, D_MODEL)
    y_sample = streams[1]["x"].reshape(nb_s, len_s, D_MODEL)
    return (y_prompt, y_sample, jnp.stack(new_ckv, axis=1), jnp.stack(new_kpe, axis=1))
```

```python
import functools
import math

import jax
import jax.numpy as jnp
from jax import lax
from jax.experimental import pallas as pl
from jax.experimental.pallas import tpu as pltpu

F32 = jnp.float32
MM_DTYPE = jnp.bfloat16

D_MODEL = 1024
DEPTH = 4
GRID_W = 64
N_MIXERS = 3

MLA_HEADS = 16
QK_NOPE_DIM = 64
QK_ROPE_DIM = 32
V_HEAD_DIM = 64
Q_LORA_RANK = 384
KV_LORA_RANK = 256
ROPE_THETA = 10000.0
HEAD_PAD = 128
ROPE_OFF = QK_NOPE_DIM

POOL_WINDOWS = (2, 4, 8, 16)
N_POOL_GROUPS = 4
POOL_GROUP_DIM = D_MODEL // N_POOL_GROUPS
POOL_HALO = 8

N_FOURIER_GROUPS = 4
FOURIER_GROUP_DIM = D_MODEL // N_FOURIER_GROUPS

PEER_HEADS = 8
PEER_N_KEYS = 128
PEER_N_EXPERTS = PEER_N_KEYS * PEER_N_KEYS
PEER_QUERY_DIM = 256
PEER_HALF = PEER_QUERY_DIM // 2
PEER_TOPK = 16

DEEPNORM_ALPHA = (2 * DEPTH) ** 0.25
LN_EPS = 1e-5
RMS_EPS = 1e-6

LANES = 128
SUBLANES = 8
N_COND_ROWS = 16
VMEM_LIMIT = 56 * 1024 * 1024

TOKEN_TILE = 512
ATTN_Q_TILE = 512
FNET_OUT_TILE = 512
FNET_IN_TILE = 1024
PEER_TOKEN_TILE = 1024
PEER_EXPERT_SLAB = 1024


def _nn(a, b):
    return jnp.dot(a, b, preferred_element_type=F32)


def _nt(a, b):
    return lax.dot_general(a, b, (((1,), (1,)), ((), ())), preferred_element_type=F32)


def _params(*sem, flags=None):
    return pltpu.CompilerParams(dimension_semantics=sem, vmem_limit_bytes=VMEM_LIMIT, flags=flags)


def _layer_norm(z, g, b):
    mu = jnp.mean(z, axis=-1, keepdims=True)
    d = z - mu
    var = jnp.mean(d * d, axis=-1, keepdims=True)
    return d * lax.rsqrt(var + LN_EPS) * g + b


def _rms_norm(x, g):
    return x * lax.rsqrt(jnp.mean(x * x, axis=-1, keepdims=True) + RMS_EPS) * g


def _full(shape):
    n = len(shape)
    return pl.BlockSpec(shape, lambda *_: (0,) * n)


def _mod_spec(cond_fn):
    return pl.BlockSpec((1, 6, D_MODEL), lambda i, *_: (cond_fn(i), 0, 0))


def _cond_fn(per_seq, tile, seq_len):
    if not per_seq:
        return lambda i: 0
    return lambda i: 1 + (i * tile) // seq_len


def _adaln_kernel(cond_ref, w_ref, b_ref, o_ref):
    c = cond_ref[...]
    c = c / (1.0 + jnp.exp(-c))
    o_ref[0] = jnp.dot(c, w_ref[0], preferred_element_type=F32,
                       precision=lax.Precision.HIGHEST) + b_ref[0]


def _adaln(cond, ada_w, ada_b):
    nb = 1536
    out = pl.pallas_call(
        _adaln_kernel,
        out_shape=jax.ShapeDtypeStruct((DEPTH, N_COND_ROWS, 6 * D_MODEL), F32),
        grid=(DEPTH, 6 * D_MODEL // nb),
        in_specs=[
            pl.BlockSpec((N_COND_ROWS, D_MODEL), lambda i, j: (0, 0)),
            pl.BlockSpec((1, D_MODEL, nb), lambda i, j: (i, 0, j)),
            pl.BlockSpec((1, 1, nb), lambda i, j: (i, 0, j)),
        ],
        out_specs=pl.BlockSpec((1, N_COND_ROWS, nb), lambda i, j: (i, 0, j)),
        compiler_params=_params("parallel", "parallel"),
        name="adaln",
    )(cond, ada_w, ada_b.reshape(DEPTH, 1, 6 * D_MODEL))
    return out.reshape(DEPTH, N_COND_ROWS, 6, D_MODEL)


def _mla_proj_kernel(x_ref, mod_ref, cos_ref, sin_ref, wdq_ref, gq_ref, wuq_ref, wuqs_ref,
                     wdkv_ref, gkv_ref, wukvk_ref, wukvv_ref,
                     q_ref, k_ref, v_ref, ckv_ref, kpe_ref):
    h = x_ref[...] * (1.0 + mod_ref[0, 1:2, :]) + mod_ref[0, 0:1, :]
    hb = h.astype(MM_DTYPE)
    cq = _rms_norm(_nn(hb, wdq_ref[...]), gq_ref[...]).astype(MM_DTYPE)
    q = _nn(cq, wuq_ref[...])
    qs = _nn(cq, wuqs_ref[...])
    kv = _nn(hb, wdkv_ref[...])
    ckv = _rms_norm(kv[:, :KV_LORA_RANK], gkv_ref[...])
    kpe = kv[:, KV_LORA_RANK:KV_LORA_RANK + HEAD_PAD]
    kpes = kv[:, KV_LORA_RANK + HEAD_PAD:]
    cos = cos_ref[...]
    sin = sin_ref[...]
    kpe_rot = kpe * cos + kpes * sin
    ckvb = ckv.astype(MM_DTYPE)
    kn = _nn(ckvb, wukvk_ref[...])
    scale = (QK_NOPE_DIM + QK_ROPE_DIM) ** -0.5 * math.log2(math.e)
    for hd in range(MLA_HEADS):
        sl = slice(hd * HEAD_PAD, (hd + 1) * HEAD_PAD)
        q_ref[:, sl] = ((q[:, sl] * cos + qs[:, sl] * sin) * scale).astype(q_ref.dtype)
        k_ref[:, sl] = (kn[:, sl] + kpe_rot).astype(k_ref.dtype)
    v_ref[0], v_ref[1] = _split_values(_nn(ckvb, wukvv_ref[...]), v_ref.dtype)
    ckv_ref[...] = ckv
    kpe_ref[...] = kpe[:, ROPE_OFF:ROPE_OFF + QK_ROPE_DIM]


def _mla_proj(x, mod, cond_fn, cos, sin, pos_fn, w, tile):
    n = x.shape[0]
    hp = MLA_HEADS * HEAD_PAD
    hv = MLA_HEADS * V_HEAD_DIM
    row = lambda i: (i, 0)
    return pl.pallas_call(
        _mla_proj_kernel,
        out_shape=(
            jax.ShapeDtypeStruct((n, hp), MM_DTYPE),
            jax.ShapeDtypeStruct((n, hp), MM_DTYPE),
            jax.ShapeDtypeStruct((2, n, hv), MM_DTYPE),
            jax.ShapeDtypeStruct((n, KV_LORA_RANK), F32),
            jax.ShapeDtypeStruct((n, QK_ROPE_DIM), F32),
        ),
        grid=(n // tile,),
        in_specs=[
            pl.BlockSpec((tile, D_MODEL), row),
            _mod_spec(cond_fn),
            pl.BlockSpec((tile, HEAD_PAD), lambda i: (pos_fn(i), 0)),
            pl.BlockSpec((tile, HEAD_PAD), lambda i: (pos_fn(i), 0)),
            _full(w["wdq"].shape), _full(w["gq"].shape), _full(w["wuq"].shape), _full(w["wuqs"].shape),
            _full(w["wdkv"].shape), _full(w["gkv"].shape), _full(w["wukvk"].shape), _full(w["wukvv"].shape),
        ],
        out_specs=(
            pl.BlockSpec((tile, hp), row),
            pl.BlockSpec((tile, hp), row),
            pl.BlockSpec((2, tile, hv), lambda i: (0, i, 0)),
            pl.BlockSpec((tile, KV_LORA_RANK), row),
            pl.BlockSpec((tile, QK_ROPE_DIM), row),
        ),
        compiler_params=_params("parallel"),
        name="mla_proj",
    )(x, mod, cos, sin, w["wdq"], w["gq"], w["wuq"], w["wuqs"], w["wdkv"], w["gkv"], w["wukvk"], w["wukvv"])


def _mla_ctx_kernel(ckv_ref, kpe_ref, wukvk_ref, wukvv_ref, place_ref, k_ref, v_ref):
    ckvb = ckv_ref[...].astype(MM_DTYPE)
    kn = _nn(ckvb, wukvk_ref[...])
    kpe = _nn(kpe_ref[...].astype(MM_DTYPE), place_ref[...])
    for hd in range(MLA_HEADS):
        sl = slice(hd * HEAD_PAD, (hd + 1) * HEAD_PAD)
        k_ref[:, sl] = (kn[:, sl] + kpe).astype(k_ref.dtype)
    v_ref[0], v_ref[1] = _split_values(_nn(ckvb, wukvv_ref[...]), v_ref.dtype)


def _mla_ctx(cache_ckv, cache_kpe, j, w):
    nb, _, past, _ = cache_ckv.shape
    hp = MLA_HEADS * HEAD_PAD
    hv = MLA_HEADS * V_HEAD_DIM
    return pl.pallas_call(
        _mla_ctx_kernel,
        out_shape=(jax.ShapeDtypeStruct((nb * past, hp), MM_DTYPE),
                   jax.ShapeDtypeStruct((2, nb * past, hv), MM_DTYPE)),
        grid=(nb,),
        in_specs=[
            pl.BlockSpec((None, None, past, KV_LORA_RANK), lambda b: (b, j, 0, 0)),
            pl.BlockSpec((None, None, past, QK_ROPE_DIM), lambda b: (b, j, 0, 0)),
            _full(w["wukvk"].shape), _full(w["wukvv"].shape), _full(w["place"].shape),
        ],
        out_specs=(pl.BlockSpec((past, hp), lambda b: (b, 0)),
                   pl.BlockSpec((2, past, hv), lambda b: (0, b, 0))),
        compiler_params=_params("parallel"),
        name="mla_ctx",
    )(cache_ckv, cache_kpe, w["wukvk"], w["wukvv"], w["place"])


def _attn_kernel(*refs, has_ctx):
    if has_ctx:
        q_ref, k_ref, v_ref, kc_ref, vc_ref, o_ref = refs
    else:
        q_ref, k_ref, v_ref, o_ref = refs
    scores = []
    for hh in range(2):
        sl = slice(hh * HEAD_PAD, (hh + 1) * HEAD_PAD)
        q = q_ref[:, sl]
        s = _nt(q, k_ref[:, sl])
        m = jnp.max(s, axis=-1, keepdims=True)
        sc = None
        if has_ctx:
            sc = _nt(q, kc_ref[:, sl])
            m = jnp.maximum(m, jnp.max(sc, axis=-1, keepdims=True))
        scores.append((s, sc, m))
    probs = []
    for s, sc, m in scores:
        p = jnp.exp2(s - m).astype(MM_DTYPE)
        pc = jnp.exp2(sc - m).astype(MM_DTYPE) if has_ctx else None
        probs.append((p, pc))
    outs = []
    for hh, (p, pc) in enumerate(probs):
        o = _nn(p, v_ref[hh])
        if has_ctx:
            o = o + _nn(pc, vc_ref[hh])
        ol = _ones_lane(hh)
        outs.append(o * (1.0 / o[:, ol:ol + 1]))
    lane = lax.broadcasted_iota(jnp.int32, (1, 2 * V_HEAD_DIM), 1)
    o_ref[...] = jnp.where(lane < V_HEAD_DIM, outs[0], outs[1]).astype(o_ref.dtype)


def _ones_lane(hh):
    return V_HEAD_DIM if hh == 0 else 0


def _split_values(v, dtype):
    lane = lax.broadcasted_iota(jnp.int32, (1, 2 * V_HEAD_DIM), 1)
    parts = []
    for hh in range(2):
        own = (lane < V_HEAD_DIM) if hh == 0 else (lane >= V_HEAD_DIM)
        fill = jnp.where(lane == _ones_lane(hh), 1.0, 0.0)
        blocks = [jnp.where(own, v[:, p * 2 * V_HEAD_DIM:(p + 1) * 2 * V_HEAD_DIM], fill)
                  for p in range(MLA_HEADS // 2)]
        parts.append(jnp.concatenate(blocks, axis=-1).astype(dtype))
    return parts


def _attention(q, k, v, n_seq, seq_len, tq, ctx=None):
    n = q.shape[0]
    hpairs = MLA_HEADS // 2
    qb = seq_len // tq
    in_specs = [
        pl.BlockSpec((tq, 2 * HEAD_PAD), lambda b, h, i: (b * qb + i, h)),
        pl.BlockSpec((seq_len, 2 * HEAD_PAD), lambda b, h, i: (b, h)),
        pl.BlockSpec((2, seq_len, 2 * V_HEAD_DIM), lambda b, h, i: (0, b, h)),
    ]
    args = [q, k, v]
    if ctx is not None:
        kc, vc = ctx
        past = kc.shape[0] // n_seq
        in_specs += [
            pl.BlockSpec((past, 2 * HEAD_PAD), lambda b, h, i: (b, h)),
            pl.BlockSpec((2, past, 2 * V_HEAD_DIM), lambda b, h, i: (0, b, h)),
        ]
        args += [kc, vc]
    return pl.pallas_call(
        functools.partial(_attn_kernel, has_ctx=ctx is not None),
        out_shape=jax.ShapeDtypeStruct((n, MLA_HEADS * V_HEAD_DIM), MM_DTYPE),
        grid=(n_seq, hpairs, qb),
        in_specs=in_specs,
        out_specs=pl.BlockSpec((tq, 2 * V_HEAD_DIM), lambda b, h, i: (b * qb + i, h)),
        compiler_params=_params("parallel", "parallel", "arbitrary"),
        name="mla_attn",
    )(*args)


def _post_store(z, mod_ref, lng_ref, lnb_ref, x1_ref, h2_ref):
    x1 = _layer_norm(z, lng_ref[...], lnb_ref[...])
    x1_ref[...] = x1
    h2_ref[...] = (x1 * (1.0 + mod_ref[0, 4:5, :]) + mod_ref[0, 3:4, :]).astype(h2_ref.dtype)


def _mla_post_kernel(o_ref, x_ref, mod_ref, wo_ref, lng_ref, lnb_ref, x1_ref, h2_ref):
    y = _nn(o_ref[...], wo_ref[...])
    z = DEEPNORM_ALPHA * x_ref[...] + mod_ref[0, 2:3, :] * y
    _post_store(z, mod_ref, lng_ref, lnb_ref, x1_ref, h2_ref)


def _mla_post(o, x, mod, cond_fn, wo, lng, lnb, tile):
    n = x.shape[0]
    row = lambda i: (i, 0)
    return pl.pallas_call(
        _mla_post_kernel,
        out_shape=(jax.ShapeDtypeStruct((n, D_MODEL), F32), jax.ShapeDtypeStruct((n, D_MODEL), MM_DTYPE)),
        grid=(n // tile,),
        in_specs=[
            pl.BlockSpec((tile, D_MODEL), row), pl.BlockSpec((tile, D_MODEL), row), _mod_spec(cond_fn),
            _full(wo.shape), _full(lng.shape), _full(lnb.shape),
        ],
        out_specs=(pl.BlockSpec((tile, D_MODEL), row), pl.BlockSpec((tile, D_MODEL), row)),
        compiler_params=_params("parallel"),
        name="mla_post",
    )(o, x, mod, wo, lng, lnb)


def _pool_kernel(x_ref, xp_ref, xn_ref, mod_ref, w_ref, ps_ref, lng_ref, lnb_ref, x1_ref, h2_ref, ext_ref,
                 *, tile, tiles_per_seq):
    i = pl.program_id(0)
    pos = i % tiles_per_seq
    scale1 = 1.0 + mod_ref[0, 1:2, :]
    shift1 = mod_ref[0, 0:1, :]
    x = x_ref[...]
    keep_prev = (pos > 0).astype(F32)
    keep_next = (pos < tiles_per_seq - 1).astype(F32)
    ext_ref[0:POOL_HALO, :] = (xp_ref[...] * scale1 + shift1) * keep_prev
    ext_ref[POOL_HALO:POOL_HALO + tile, :] = x * scale1 + shift1
    ext_ref[POOL_HALO + tile:, :] = (xn_ref[...] * scale1 + shift1) * keep_next
    seq_len = tile * tiles_per_seq
    t = pos * tile + lax.broadcasted_iota(jnp.int32, (tile, POOL_GROUP_DIM), 0)
    ys = []
    for g, w in enumerate(POOL_WINDOWS):
        cs = slice(g * POOL_GROUP_DIM, (g + 1) * POOL_GROUP_DIM)
        tot = None
        for d in range(-(w // 2), w - w // 2):
            piece = ext_ref[POOL_HALO + d:POOL_HALO + d + tile, cs]
            tot = piece if tot is None else tot + piece
        lo = jnp.maximum(t - w // 2, 0)
        hi = jnp.minimum(t + (w - w // 2) - 1, seq_len - 1)
        cnt = (hi - lo + 1).astype(F32)
        diff = tot / cnt - ext_ref[POOL_HALO:POOL_HALO + tile, cs]
        ys.append(_nn(diff.astype(MM_DTYPE), w_ref[g]))
    y = jnp.concatenate(ys, axis=-1) * ps_ref[...]
    z = DEEPNORM_ALPHA * x + mod_ref[0, 2:3, :] * y
    _post_store(z, mod_ref, lng_ref, lnb_ref, x1_ref, h2_ref)


def _pool_layer(x, mod, cond_fn, seq_len, w, ps, lng, lnb, tile):
    n = x.shape[0]
    tps = seq_len // tile
    hb = tile // POOL_HALO
    nblk = n // POOL_HALO
    row = lambda i: (i, 0)
    return pl.pallas_call(
        functools.partial(_pool_kernel, tile=tile, tiles_per_seq=tps),
        out_shape=(jax.ShapeDtypeStruct((n, D_MODEL), F32), jax.ShapeDtypeStruct((n, D_MODEL), MM_DTYPE)),
        grid=(n // tile,),
        in_specs=[
            pl.BlockSpec((tile, D_MODEL), row),
            pl.BlockSpec((POOL_HALO, D_MODEL), lambda i: (jnp.maximum(i * hb - 1, 0), 0)),
            pl.BlockSpec((POOL_HALO, D_MODEL), lambda i: (jnp.minimum((i + 1) * hb, nblk - 1), 0)),
            _mod_spec(cond_fn), _full(w.shape), _full(ps.shape), _full(lng.shape), _full(lnb.shape),
        ],
        out_specs=(pl.BlockSpec((tile, D_MODEL), row), pl.BlockSpec((tile, D_MODEL), row)),
        scratch_shapes=[pltpu.VMEM((tile + 2 * POOL_HALO, D_MODEL), F32)],
        compiler_params=_params("parallel"),
        name="pool_layer",
    )(x, x, x, mod, w, ps, lng, lnb)


def _fnet_chan_kernel(x_ref, mod_ref, cc_ref, sc_ref, hc_ref, hs_ref):
    h = x_ref[...] * (1.0 + mod_ref[0, 1:2, :]) + mod_ref[0, 0:1, :]
    hb = h.astype(MM_DTYPE)
    for g in range(N_FOURIER_GROUPS):
        cs = slice(g * FOURIER_GROUP_DIM, (g + 1) * FOURIER_GROUP_DIM)
        hc_ref[:, cs] = _nn(hb[:, cs], cc_ref[...]).astype(hc_ref.dtype)
        hs_ref[:, cs] = _nn(hb[:, cs], sc_ref[...]).astype(hs_ref.dtype)


def _fnet_chan(x, mod, cond_fn, cc, sc, tile):
    n = x.shape[0]
    row = lambda i: (i, 0)
    return pl.pallas_call(
        _fnet_chan_kernel,
        out_shape=(jax.ShapeDtypeStruct((n, D_MODEL), MM_DTYPE), jax.ShapeDtypeStruct((n, D_MODEL), MM_DTYPE)),
        grid=(n // tile,),
        in_specs=[pl.BlockSpec((tile, D_MODEL), row), _mod_spec(cond_fn), _full(cc.shape), _full(sc.shape)],
        out_specs=(pl.BlockSpec((tile, D_MODEL), row), pl.BlockSpec((tile, D_MODEL), row)),
        compiler_params=_params("parallel"),
        name="fnet_chan",
    )(x, mod, cc, sc)


def _fnet_seq_kernel(cl_ref, sl_ref, hc_ref, hs_ref, x_ref, mod_ref, fw_ref, fb_ref, lng_ref, lnb_ref,
                     x1_ref, h2_ref, acc_ref):
    tt = pl.program_id(2)

    @pl.when(tt == 0)
    def _():
        acc_ref[...] = jnp.zeros_like(acc_ref)

    acc_ref[...] += _nn(cl_ref[...], hc_ref[...]) - _nn(sl_ref[...], hs_ref[...])

    @pl.when(tt == pl.num_programs(2) - 1)
    def _():
        y = _nn(acc_ref[...].astype(MM_DTYPE), fw_ref[...]) + fb_ref[...]
        z = DEEPNORM_ALPHA * x_ref[...] + mod_ref[0, 2:3, :] * y
        _post_store(z, mod_ref, lng_ref, lnb_ref, x1_ref, h2_ref)


def _fnet_seq(hc, hs, x, mod, per_seq, n_seq, seq_len, cl, sl, fw, fb, lng, lnb, tk, tt):
    n = x.shape[0]
    kb = seq_len // tk
    tb = seq_len // tt
    out_row = lambda b, k, t: (b * kb + k, 0)
    cond = (lambda b, k, t: (1 + b, 0, 0)) if per_seq else (lambda b, k, t: (0, 0, 0))
    return pl.pallas_call(
        _fnet_seq_kernel,
        out_shape=(jax.ShapeDtypeStruct((n, D_MODEL), F32), jax.ShapeDtypeStruct((n, D_MODEL), MM_DTYPE)),
        grid=(n_seq, kb, tb),
        in_specs=[
            pl.BlockSpec((tk, tt), lambda b, k, t: (k, t)),
            pl.BlockSpec((tk, tt), lambda b, k, t: (k, t)),
            pl.BlockSpec((tt, D_MODEL), lambda b, k, t: (b * tb + t, 0)),
            pl.BlockSpec((tt, D_MODEL), lambda b, k, t: (b * tb + t, 0)),
            pl.BlockSpec((tk, D_MODEL), out_row),
            pl.BlockSpec((1, 6, D_MODEL), cond),
            _full(fw.shape), _full(fb.shape), _full(lng.shape), _full(lnb.shape),
        ],
        out_specs=(pl.BlockSpec((tk, D_MODEL), out_row), pl.BlockSpec((tk, D_MODEL), out_row)),
        scratch_shapes=[pltpu.VMEM((tk, D_MODEL), F32)],
        compiler_params=_params("parallel", "parallel", "arbitrary"),
        name="fnet_seq",
    )(cl, sl, hc, hs, x, mod, fw, fb, lng, lnb)


def _top16(x):
    v = _sort_desc([x[SUBLANES * r:SUBLANES * (r + 1)] for r in range(x.shape[0] // SUBLANES)])
    rows = []
    for k in range(PEER_TOPK):
        m = jnp.max(v[0], axis=0, keepdims=True)
        rows.append(m)
        hit = v[0] == m
        v = [jnp.where(hit, v[d + 1], v[d]) for d in range(PEER_TOPK - 1 - k)]
    return rows


def _sort_desc(v):
    v = list(v)
    n = len(v)
    p = 1
    while p < n:
        k = p
        while k >= 1:
            for j in range(k % p, n - k, 2 * k):
                for i in range(min(k, n - j - k)):
                    if (i + j) // (2 * p) == (i + j + k) // (2 * p):
                        a, b = v[i + j], v[i + j + k]
                        v[i + j], v[i + j + k] = jnp.maximum(a, b), jnp.minimum(a, b)
            k //= 2
        p *= 2
    return v


def _stack8(rows, row8):
    out = jnp.broadcast_to(rows[0], row8.shape)
    for k in range(1, 8):
        out = jnp.where(row8 == k, rows[k], out)
    return out


def _peer_topk_kernel(h_ref, wqt_ref, sk_ref, e2_ref, th_ref, e1_ref, qt_ref, *, tile):
    qt_ref[...] = _nt(wqt_ref[...], h_ref[...]).astype(qt_ref.dtype)
    row8 = lax.broadcasted_iota(jnp.int32, (8, LANES), 0)

    def head(hd, carry):
        base = pl.multiple_of(hd * PEER_QUERY_DIM, PEER_QUERY_DIM)
        for lg in range(tile // LANES):
            ls = slice(lg * LANES, (lg + 1) * LANES)
            s1 = _nn(sk_ref[2 * hd], qt_ref[pl.ds(base, PEER_HALF), ls])
            s2 = _nn(sk_ref[2 * hd + 1], qt_ref[pl.ds(base + PEER_HALF, PEER_HALF), ls])
            t1 = _top16(s1)
            t2 = _top16(s2)
            t1a = _stack8(t1[:8], row8)
            lists = [t1a + t2[0]] + [jnp.where(row8 < PEER_TOPK // (l + 1), t1a + t2[l], -jnp.inf)
                                     for l in range(1, PEER_TOPK)]
            tail = _stack8(t1[8:], row8) + t2[0]
            top = t1[0] + t2[0]
            z = jnp.zeros_like(top)
            tau = top
            for j in range(PEER_TOPK):
                tau = jnp.max(jnp.maximum(lists[0], tail), axis=0, keepdims=True)
                z = z + jnp.exp(tau - top)
                if j < PEER_TOPK - 1:
                    hit = lists[0] == tau
                    tail = jnp.where(tail == tau, -jnp.inf, tail)
                    lists = [jnp.where(hit, lists[d + 1], lists[d]) for d in range(len(lists) - 1)]
            e2_of = lambda s: jnp.exp(s - t2[0]) * (0.5 / z)
            theta = jnp.full_like(s1, jnp.inf)
            for l in range(PEER_TOPK // 2):
                theta = jnp.where(s1 + t2[l] >= tau, e2_of(t2[l]), theta)
            best = jnp.full_like(top, jnp.inf)
            for l in range(PEER_TOPK // 2, PEER_TOPK):
                best = jnp.where(t1[0] + t2[l] >= tau, e2_of(t2[l]), best)
            theta = jnp.where(s1 == t1[0], jnp.minimum(theta, best), theta)
            e2_ref[hd, lg] = e2_of(s2)
            th_ref[hd, lg] = theta
            e1_ref[hd, lg] = jnp.exp(s1 - t1[0])
        return carry

    lax.fori_loop(0, PEER_HEADS, head, 0)


def _peer_topk(h2, wqt, sk, tile):
    n = h2.shape[0]
    g = tile // LANES
    shp = jax.ShapeDtypeStruct((PEER_HEADS, n // LANES, PEER_N_KEYS, LANES), F32)
    spec = pl.BlockSpec((PEER_HEADS, g, PEER_N_KEYS, LANES), lambda i: (0, i, 0, 0))
    return pl.pallas_call(
        functools.partial(_peer_topk_kernel, tile=tile),
        out_shape=(shp, shp, shp),
        grid=(n // tile,),
        in_specs=[pl.BlockSpec((tile, D_MODEL), lambda i: (i, 0)), _full(wqt.shape), _full(sk.shape)],
        out_specs=(spec, spec, spec),
        scratch_shapes=[pltpu.VMEM((PEER_HEADS * PEER_QUERY_DIM, tile), MM_DTYPE)],
        compiler_params=_params("parallel"),
        name="peer_topk",
    )(h2, wqt, sk)


def _peer_main_kernel(h_ref, u_ref, vt_ref, e2_ref, th_ref, e1_ref, x1_ref, mod_ref, lng_ref, lnb_ref,
                      out_ref, at_ref, hm_ref, acc_ref, *, tile, eblk):
    j = pl.program_id(1)

    @pl.when(j == 0)
    def _():
        acc_ref[...] = jnp.zeros_like(acc_ref)

    at_ref[...] = _nt(u_ref[...], h_ref[...])
    for lg in range(tile // LANES):
        ls = slice(lg * LANES, (lg + 1) * LANES)
        for il in range(eblk // PEER_N_KEYS):
            rs = slice(il * PEER_N_KEYS, (il + 1) * PEER_N_KEYS)
            gate = None
            for hd in range(PEER_HEADS):
                e2 = e2_ref[hd, lg]
                w = jnp.where(e2 >= th_ref[hd, lg, il:il + 1, :], e1_ref[hd, lg, il:il + 1, :] * e2, 0.0)
                gate = w if gate is None else gate + w
            a = at_ref[rs, ls]
            hm_ref[rs, ls] = (gate * a * (1.0 + lax.erf(a * (2.0 ** -0.5)))).astype(hm_ref.dtype)
    acc_ref[...] += _nn(vt_ref[...], hm_ref[...])

    @pl.when(j == pl.num_programs(1) - 1)
    def _():
        f = acc_ref[...].T
        z = DEEPNORM_ALPHA * x1_ref[...] + mod_ref[0, 5:6, :] * f
        out_ref[...] = _layer_norm(z, lng_ref[...], lnb_ref[...])


def _peer_main(h2, u, vt, e2, th, e1, x1, mod, cond_fn, lng, lnb, tile, eblk):
    n = h2.shape[0]
    g = tile // LANES
    ik = eblk // PEER_N_KEYS
    row = lambda i, j: (i, 0)
    full_keys = pl.BlockSpec((PEER_HEADS, g, PEER_N_KEYS, LANES), lambda i, j: (0, i, 0, 0))
    slab_keys = pl.BlockSpec((PEER_HEADS, g, ik, LANES), lambda i, j: (0, i, j, 0))
    return pl.pallas_call(
        functools.partial(_peer_main_kernel, tile=tile, eblk=eblk),
        out_shape=jax.ShapeDtypeStruct((n, D_MODEL), F32),
        grid=(n // tile, PEER_N_EXPERTS // eblk),
        in_specs=[
            pl.BlockSpec((tile, D_MODEL), row),
            pl.BlockSpec((eblk, D_MODEL), lambda i, j: (j, 0)),
            pl.BlockSpec((D_MODEL, eblk), lambda i, j: (0, j)),
            full_keys, slab_keys, slab_keys,
            pl.BlockSpec((tile, D_MODEL), row),
            pl.BlockSpec((1, 6, D_MODEL), lambda i, j: (cond_fn(i), 0, 0)),
            pl.BlockSpec(lng.shape, lambda i, j: (0, 0)), pl.BlockSpec(lnb.shape, lambda i, j: (0, 0)),
        ],
        out_specs=pl.BlockSpec((tile, D_MODEL), row),
        scratch_shapes=[pltpu.VMEM((eblk, tile), F32), pltpu.VMEM((eblk, tile), MM_DTYPE),
                        pltpu.VMEM((D_MODEL, tile), F32)],
        compiler_params=_params("parallel", "arbitrary"),
        name="peer_main",
    )(h2, u, vt, e2, th, e1, x1, mod, lng, lnb)


def _peer_layer(h2, x1, mod, cond_fn, pw, lng, lnb, tile_topk, tile_main, eblk):
    e2, th, e1 = _peer_topk(h2, pw["wqt"], pw["sk"], tile_topk)
    return _peer_main(h2, pw["u"], pw["vt"], e2, th, e1, x1, mod, cond_fn, lng, lnb, tile_main, eblk)


def _pad_heads(w, n_used):
    k = w.shape[0]
    return jnp.pad(w, ((0, 0), (0, 0), (0, HEAD_PAD - n_used))).reshape(k, MLA_HEADS * HEAD_PAD)


def _rotate_half_cols(w):
    n = QK_ROPE_DIM // 4
    w4 = w.reshape(w.shape[:-1] + (2, 2, n))
    return jnp.stack([-w4[..., 1, :], w4[..., 0, :]], axis=-2).reshape(w.shape)


def _mla_weights(wdq, gq, wuq, wdkv, gkv, wukv):
    dq = QK_NOPE_DIM + QK_ROPE_DIM
    wuq3 = wuq.reshape(Q_LORA_RANK, MLA_HEADS, dq)
    pe = wuq3[..., QK_NOPE_DIM:]
    zeros_nope = jnp.zeros((Q_LORA_RANK, MLA_HEADS, QK_NOPE_DIM), F32)
    wuq_pad = _pad_heads(wuq3, dq)
    wuqs_pad = _pad_heads(jnp.concatenate([zeros_nope, _rotate_half_cols(pe)], axis=-1), dq)
    wkpe = wdkv[:, KV_LORA_RANK:]
    zk = jnp.zeros((D_MODEL, ROPE_OFF), F32)
    zt = jnp.zeros((D_MODEL, HEAD_PAD - ROPE_OFF - QK_ROPE_DIM), F32)
    wdkv_pad = jnp.concatenate([wdkv[:, :KV_LORA_RANK], zk, wkpe, zt, zk, _rotate_half_cols(wkpe), zt], axis=-1)
    wukv3 = wukv.reshape(KV_LORA_RANK, MLA_HEADS, QK_NOPE_DIM + V_HEAD_DIM)
    place = jnp.pad(jnp.eye(QK_ROPE_DIM, dtype=F32), ((0, 0), (ROPE_OFF, HEAD_PAD - ROPE_OFF - QK_ROPE_DIM)))
    return dict(
        wdq=wdq.astype(MM_DTYPE), gq=gq.reshape(1, -1), wuq=wuq_pad.astype(MM_DTYPE), wuqs=wuqs_pad.astype(MM_DTYPE),
        wdkv=wdkv_pad.astype(MM_DTYPE), gkv=gkv.reshape(1, -1),
        wukvk=_pad_heads(wukv3[..., :QK_NOPE_DIM], QK_NOPE_DIM).astype(MM_DTYPE),
        wukvv=wukv3[..., QK_NOPE_DIM:].reshape(KV_LORA_RANK, MLA_HEADS * V_HEAD_DIM).astype(MM_DTYPE),
        place=place.astype(MM_DTYPE),
    )


def _rope_tables(length):
    rows = length // GRID_W
    t = jnp.arange(rows * GRID_W)
    row = (t // GRID_W).astype(F32)
    col = (t % GRID_W).astype(F32)
    half = QK_ROPE_DIM // 2
    inv_freq = 1.0 / jnp.power(ROPE_THETA, jnp.arange(0, half, 2, dtype=F32) / half)
    n = QK_ROPE_DIM // 4
    ang = jnp.stack([row[:, None] * inv_freq, col[:, None] * inv_freq], axis=1)
    ang = jnp.broadcast_to(ang[:, :, None, :], (length, 2, 2, n)).reshape(length, QK_ROPE_DIM)
    pad = ((0, 0), (ROPE_OFF, HEAD_PAD - ROPE_OFF - QK_ROPE_DIM))
    cos = jnp.pad(jnp.cos(ang), pad, constant_values=1.0)
    sin = jnp.pad(jnp.sin(ang), pad)
    return cos, sin


def _dft_tables(length, scale):
    k = jnp.arange(length, dtype=jnp.int32)
    kt = (k[:, None] * k[None, :]) % length
    ang = kt.astype(F32) * (2.0 * math.pi / length)
    return (jnp.cos(ang) * scale).astype(MM_DTYPE), (jnp.sin(ang) * scale).astype(MM_DTYPE)


def kernel(x_prompt, x_sample, cache_ckv, cache_kpe, c, c_ctx, ada_w, ada_b, ln_mix_g, ln_mix_b, ln_ffn_g, ln_ffn_b, mla_wdq, mla_q_norm_g, mla_wuq, mla_wdkv, mla_kv_norm_g, mla_wukv, mla_wo, pool_w, pool_scale, fnet_w, fnet_b, peer_wq, peer_subkeys, peer_u, peer_v):
    nb_p, len_p, _ = x_prompt.shape
    nb_s, len_s, _ = x_sample.shape
    assert 1 + nb_s <= N_COND_ROWS
    cond = jnp.zeros((N_COND_ROWS, D_MODEL), F32).at[0].set(c_ctx).at[1:1 + nb_s].set(c)
    mods = _adaln(cond, ada_w, ada_b)

    streams = [
        dict(x=x_prompt.reshape(nb_p * len_p, D_MODEL), n_seq=nb_p, seq_len=len_p, per_seq=False),
        dict(x=x_sample.reshape(nb_s * len_s, D_MODEL), n_seq=nb_s, seq_len=len_s, per_seq=True),
    ]
    cos_s, sin_s = _rope_tables(len_s)
    new_ckv, new_kpe = [], []

    for i in range(DEPTH):
        kind, j = i % N_MIXERS, i // N_MIXERS
        mod = mods[i]
        lng, lnb = ln_mix_g[i].reshape(1, -1), ln_mix_b[i].reshape(1, -1)
        pw = dict(
            wqt=peer_wq[i].T.astype(MM_DTYPE),
            sk=peer_subkeys[i].reshape(PEER_HEADS * 2, PEER_N_KEYS, PEER_HALF).astype(MM_DTYPE),
            u=peer_u[i].astype(MM_DTYPE),
            vt=peer_v[i].T.astype(MM_DTYPE),
        )
        if kind == 0:
            mw = _mla_weights(mla_wdq[j], mla_q_norm_g[j], mla_wuq[j], mla_wdkv[j], mla_kv_norm_g[j], mla_wukv[j])
            wo = mla_wo[j].astype(MM_DTYPE)
            ctx = _mla_ctx(cache_ckv, cache_kpe, j, mw)
        elif kind == 2:
            cc, sc = _dft_tables(FOURIER_GROUP_DIM, FOURIER_GROUP_DIM ** -0.5)
        for s in streams:
            x, seq_len, n_seq = s["x"], s["seq_len"], s["n_seq"]
            tile = min(TOKEN_TILE, seq_len)
            cond_fn = _cond_fn(s["per_seq"], tile, seq_len)
            if kind == 0:
                if s["per_seq"]:
                    tps = seq_len // tile
                    q, k, v, _, _ = _mla_proj(x, mod, cond_fn, cos_s, sin_s, lambda t, tps=tps: t % tps, mw, tile)
                    o = _attention(q, k, v, n_seq, seq_len, min(ATTN_Q_TILE, seq_len), ctx=ctx)
                else:
                    no_pos = (jnp.ones((tile, HEAD_PAD), F32), jnp.zeros((tile, HEAD_PAD), F32))
                    q, k, v, ckv, kpe = _mla_proj(x, mod, cond_fn, *no_pos, lambda t: 0, mw, tile)
                    o = _attention(q, k, v, n_seq, seq_len, min(ATTN_Q_TILE, seq_len))
                    new_ckv.append(ckv.reshape(n_seq, seq_len, KV_LORA_RANK))
                    new_kpe.append(kpe.reshape(n_seq, seq_len, QK_ROPE_DIM))
                x1, h2 = _mla_post(o, x, mod, cond_fn, wo, lng, lnb, tile)
            elif kind == 1:
                x1, h2 = _pool_layer(x, mod, cond_fn, seq_len, pool_w[j].astype(MM_DTYPE),
                                     pool_scale[j].reshape(1, -1), lng, lnb, tile)
            else:
                hc, hs = _fnet_chan(x, mod, cond_fn, cc, sc, tile)
                cl, sl = _dft_tables(seq_len, seq_len ** -0.5)
                x1, h2 = _fnet_seq(hc, hs, x, mod, s["per_seq"], n_seq, seq_len, cl, sl,
                                   fnet_w[j].astype(MM_DTYPE), fnet_b[j].reshape(1, -1), lng, lnb,
                                   min(FNET_OUT_TILE, seq_len), min(FNET_IN_TILE, seq_len))
            tile_main = min(PEER_TOKEN_TILE, seq_len if s["per_seq"] else x.shape[0])
            cond_main = _cond_fn(s["per_seq"], tile_main, seq_len)
            s["x"] = _peer_layer(h2, x1, mod, cond_main, pw, ln_ffn_g[i].reshape(1, -1), ln_ffn_b[i].reshape(1, -1),
                                 tile, tile_main, PEER_EXPERT_SLAB)

    y_prompt = streams[0]["x"].reshape(nb_p, len_p, D_MODEL)
    y_sample = streams[1]["x"].reshape(nb_s, len_s, D_MODEL)
    return (y_prompt, y_sample, jnp.stack(new_ckv, axis=1), jnp.stack(new_kpe, axis=1))
```

```python
import functools
import math

import jax
import jax.numpy as jnp
from jax import lax
from jax.experimental import pallas as pl
from jax.experimental.pallas import tpu as pltpu

F32 = jnp.float32
MM_DTYPE = jnp.bfloat16

D_MODEL = 1024
DEPTH = 4
GRID_W = 64
N_MIXERS = 3

MLA_HEADS = 16
QK_NOPE_DIM = 64
QK_ROPE_DIM = 32
V_HEAD_DIM = 64
Q_LORA_RANK = 384
KV_LORA_RANK = 256
ROPE_THETA = 10000.0
HEAD_PAD = 128
ROPE_OFF = QK_NOPE_DIM

POOL_WINDOWS = (2, 4, 8, 16)
N_POOL_GROUPS = 4
POOL_GROUP_DIM = D_MODEL // N_POOL_GROUPS
POOL_HALO = 8

N_FOURIER_GROUPS = 4
FOURIER_GROUP_DIM = D_MODEL // N_FOURIER_GROUPS

PEER_HEADS = 8
PEER_N_KEYS = 128
PEER_N_EXPERTS = PEER_N_KEYS * PEER_N_KEYS
PEER_QUERY_DIM = 256
PEER_HALF = PEER_QUERY_DIM // 2
PEER_TOPK = 16

DEEPNORM_ALPHA = (2 * DEPTH) ** 0.25
LN_EPS = 1e-5
RMS_EPS = 1e-6

LANES = 128
SUBLANES = 8
N_COND_ROWS = 16
VMEM_LIMIT = 56 * 1024 * 1024

TOKEN_TILE = 512
ATTN_Q_TILE = 512
FNET_OUT_TILE = 512
FNET_IN_TILE = 1024
PEER_TOPK_TILE = 1024
PEER_TOKEN_TILE = 1024
PEER_EXPERT_SLAB = 1024


def _nn(a, b):
    return jnp.dot(a, b, preferred_element_type=F32)


def _nt(a, b):
    return lax.dot_general(a, b, (((1,), (1,)), ((), ())), preferred_element_type=F32)


def _params(*sem, flags=None):
    return pltpu.CompilerParams(dimension_semantics=sem, vmem_limit_bytes=VMEM_LIMIT, flags=flags)


def _layer_norm(z, g, b):
    mu = jnp.mean(z, axis=-1, keepdims=True)
    d = z - mu
    var = jnp.mean(d * d, axis=-1, keepdims=True)
    return d * lax.rsqrt(var + LN_EPS) * g + b


def _rms_norm(x, g):
    return x * lax.rsqrt(jnp.mean(x * x, axis=-1, keepdims=True) + RMS_EPS) * g


def _full(shape):
    n = len(shape)
    return pl.BlockSpec(shape, lambda *_: (0,) * n)


def _mod_spec(cond_fn):
    return pl.BlockSpec((1, 6, D_MODEL), lambda i, *_: (cond_fn(i), 0, 0))


def _cond_fn(per_seq, tile, seq_len):
    if not per_seq:
        return lambda i: 0
    return lambda i: 1 + (i * tile) // seq_len


def _adaln_kernel(cond_ref, w_ref, b_ref, o_ref):
    c = cond_ref[...]
    c = c / (1.0 + jnp.exp(-c))
    o_ref[0] = jnp.dot(c, w_ref[0], preferred_element_type=F32,
                       precision=lax.Precision.HIGHEST) + b_ref[0]


def _adaln(cond, ada_w, ada_b):
    nb = 1536
    out = pl.pallas_call(
        _adaln_kernel,
        out_shape=jax.ShapeDtypeStruct((DEPTH, N_COND_ROWS, 6 * D_MODEL), F32),
        grid=(DEPTH, 6 * D_MODEL // nb),
        in_specs=[
            pl.BlockSpec((N_COND_ROWS, D_MODEL), lambda i, j: (0, 0)),
            pl.BlockSpec((1, D_MODEL, nb), lambda i, j: (i, 0, j)),
            pl.BlockSpec((1, 1, nb), lambda i, j: (i, 0, j)),
        ],
        out_specs=pl.BlockSpec((1, N_COND_ROWS, nb), lambda i, j: (i, 0, j)),
        compiler_params=_params("parallel", "parallel"),
        name="adaln",
    )(cond, ada_w, ada_b.reshape(DEPTH, 1, 6 * D_MODEL))
    return out.reshape(DEPTH, N_COND_ROWS, 6, D_MODEL)


def _mla_proj_kernel(x_ref, mod_ref, cos_ref, sin_ref, wdq_ref, gq_ref, wuq_ref, wuqs_ref,
                     wdkv_ref, gkv_ref, wukvk_ref, wukvv_ref,
                     q_ref, k_ref, v_ref, ckv_ref, kpe_ref):
    h = x_ref[...] * (1.0 + mod_ref[0, 1:2, :]) + mod_ref[0, 0:1, :]
    hb = h.astype(MM_DTYPE)
    cq = _rms_norm(_nn(hb, wdq_ref[...]), gq_ref[...]).astype(MM_DTYPE)
    q = _nn(cq, wuq_ref[...])
    qs = _nn(cq, wuqs_ref[...])
    kv = _nn(hb, wdkv_ref[...])
    ckv = _rms_norm(kv[:, :KV_LORA_RANK], gkv_ref[...])
    kpe = kv[:, KV_LORA_RANK:KV_LORA_RANK + HEAD_PAD]
    kpes = kv[:, KV_LORA_RANK + HEAD_PAD:]
    cos = cos_ref[...]
    sin = sin_ref[...]
    kpe_rot = kpe * cos + kpes * sin
    ckvb = ckv.astype(MM_DTYPE)
    kn = _nn(ckvb, wukvk_ref[...])
    scale = (QK_NOPE_DIM + QK_ROPE_DIM) ** -0.5 * math.log2(math.e)
    for hd in range(MLA_HEADS):
        sl = slice(hd * HEAD_PAD, (hd + 1) * HEAD_PAD)
        q_ref[:, sl] = ((q[:, sl] * cos + qs[:, sl] * sin) * scale).astype(q_ref.dtype)
        k_ref[:, sl] = (kn[:, sl] + kpe_rot).astype(k_ref.dtype)
    v_ref[0], v_ref[1] = _split_values(_nn(ckvb, wukvv_ref[...]), v_ref.dtype)
    ckv_ref[...] = ckv
    kpe_ref[...] = kpe[:, ROPE_OFF:ROPE_OFF + QK_ROPE_DIM]


def _mla_proj(x, mod, cond_fn, cos, sin, pos_fn, w, tile):
    n = x.shape[0]
    hp = MLA_HEADS * HEAD_PAD
    hv = MLA_HEADS * V_HEAD_DIM
    row = lambda i: (i, 0)
    return pl.pallas_call(
        _mla_proj_kernel,
        out_shape=(
            jax.ShapeDtypeStruct((n, hp), MM_DTYPE),
            jax.ShapeDtypeStruct((n, hp), MM_DTYPE),
            jax.ShapeDtypeStruct((2, n, hv), MM_DTYPE),
            jax.ShapeDtypeStruct((n, KV_LORA_RANK), F32),
            jax.ShapeDtypeStruct((n, QK_ROPE_DIM), F32),
        ),
        grid=(n // tile,),
        in_specs=[
            pl.BlockSpec((tile, D_MODEL), row),
            _mod_spec(cond_fn),
            pl.BlockSpec((tile, HEAD_PAD), lambda i: (pos_fn(i), 0)),
            pl.BlockSpec((tile, HEAD_PAD), lambda i: (pos_fn(i), 0)),
            _full(w["wdq"].shape), _full(w["gq"].shape), _full(w["wuq"].shape), _full(w["wuqs"].shape),
            _full(w["wdkv"].shape), _full(w["gkv"].shape), _full(w["wukvk"].shape), _full(w["wukvv"].shape),
        ],
        out_specs=(
            pl.BlockSpec((tile, hp), row),
            pl.BlockSpec((tile, hp), row),
            pl.BlockSpec((2, tile, hv), lambda i: (0, i, 0)),
            pl.BlockSpec((tile, KV_LORA_RANK), row),
            pl.BlockSpec((tile, QK_ROPE_DIM), row),
        ),
        compiler_params=_params("parallel"),
        name="mla_proj",
    )(x, mod, cos, sin, w["wdq"], w["gq"], w["wuq"], w["wuqs"], w["wdkv"], w["gkv"], w["wukvk"], w["wukvv"])


def _mla_ctx_kernel(ckv_ref, kpe_ref, wukvk_ref, wukvv_ref, place_ref, k_ref, v_ref):
    ckvb = ckv_ref[...].astype(MM_DTYPE)
    kn = _nn(ckvb, wukvk_ref[...])
    kpe = _nn(kpe_ref[...].astype(MM_DTYPE), place_ref[...])
    for hd in range(MLA_HEADS):
        sl = slice(hd * HEAD_PAD, (hd + 1) * HEAD_PAD)
        k_ref[:, sl] = (kn[:, sl] + kpe).astype(k_ref.dtype)
    v_ref[0], v_ref[1] = _split_values(_nn(ckvb, wukvv_ref[...]), v_ref.dtype)


def _mla_ctx(cache_ckv, cache_kpe, j, w):
    nb, _, past, _ = cache_ckv.shape
    hp = MLA_HEADS * HEAD_PAD
    hv = MLA_HEADS * V_HEAD_DIM
    return pl.pallas_call(
        _mla_ctx_kernel,
        out_shape=(jax.ShapeDtypeStruct((nb * past, hp), MM_DTYPE),
                   jax.ShapeDtypeStruct((2, nb * past, hv), MM_DTYPE)),
        grid=(nb,),
        in_specs=[
            pl.BlockSpec((None, None, past, KV_LORA_RANK), lambda b: (b, j, 0, 0)),
            pl.BlockSpec((None, None, past, QK_ROPE_DIM), lambda b: (b, j, 0, 0)),
            _full(w["wukvk"].shape), _full(w["wukvv"].shape), _full(w["place"].shape),
        ],
        out_specs=(pl.BlockSpec((past, hp), lambda b: (b, 0)),
                   pl.BlockSpec((2, past, hv), lambda b: (0, b, 0))),
        compiler_params=_params("parallel"),
        name="mla_ctx",
    )(cache_ckv, cache_kpe, w["wukvk"], w["wukvv"], w["place"])


def _attn_kernel(*refs, has_ctx):
    if has_ctx:
        q_ref, k_ref, v_ref, kc_ref, vc_ref, o_ref = refs
    else:
        q_ref, k_ref, v_ref, o_ref = refs
    scores = []
    for hh in range(2):
        sl = slice(hh * HEAD_PAD, (hh + 1) * HEAD_PAD)
        q = q_ref[:, sl]
        s = _nt(q, k_ref[:, sl])
        m = jnp.max(s, axis=-1, keepdims=True)
        sc = None
        if has_ctx:
            sc = _nt(q, kc_ref[:, sl])
            m = jnp.maximum(m, jnp.max(sc, axis=-1, keepdims=True))
        scores.append((s, sc, m))
    probs = []
    for s, sc, m in scores:
        p = jnp.exp2(s - m).astype(MM_DTYPE)
        pc = jnp.exp2(sc - m).astype(MM_DTYPE) if has_ctx else None
        probs.append((p, pc))
    outs = []
    for hh, (p, pc) in enumerate(probs):
        o = _nn(p, v_ref[hh])
        if has_ctx:
            o = o + _nn(pc, vc_ref[hh])
        ol = _ones_lane(hh)
        outs.append(o * (1.0 / o[:, ol:ol + 1]))
    lane = lax.broadcasted_iota(jnp.int32, (1, 2 * V_HEAD_DIM), 1)
    o_ref[...] = jnp.where(lane < V_HEAD_DIM, outs[0], outs[1]).astype(o_ref.dtype)


def _ones_lane(hh):
    return V_HEAD_DIM if hh == 0 else 0


def _split_values(v, dtype):
    lane = lax.broadcasted_iota(jnp.int32, (1, 2 * V_HEAD_DIM), 1)
    parts = []
    for hh in range(2):
        own = (lane < V_HEAD_DIM) if hh == 0 else (lane >= V_HEAD_DIM)
        fill = jnp.where(lane == _ones_lane(hh), 1.0, 0.0)
        blocks = [jnp.where(own, v[:, p * 2 * V_HEAD_DIM:(p + 1) * 2 * V_HEAD_DIM], fill)
                  for p in range(MLA_HEADS // 2)]
        parts.append(jnp.concatenate(blocks, axis=-1).astype(dtype))
    return parts


def _attention(q, k, v, n_seq, seq_len, tq, ctx=None):
    n = q.shape[0]
    hpairs = MLA_HEADS // 2
    qb = seq_len // tq
    in_specs = [
        pl.BlockSpec((tq, 2 * HEAD_PAD), lambda b, h, i: (b * qb + i, h)),
        pl.BlockSpec((seq_len, 2 * HEAD_PAD), lambda b, h, i: (b, h)),
        pl.BlockSpec((2, seq_len, 2 * V_HEAD_DIM), lambda b, h, i: (0, b, h)),
    ]
    args = [q, k, v]
    if ctx is not None:
        kc, vc = ctx
        past = kc.shape[0] // n_seq
        in_specs += [
            pl.BlockSpec((past, 2 * HEAD_PAD), lambda b, h, i: (b, h)),
            pl.BlockSpec((2, past, 2 * V_HEAD_DIM), lambda b, h, i: (0, b, h)),
        ]
        args += [kc, vc]
    return pl.pallas_call(
        functools.partial(_attn_kernel, has_ctx=ctx is not None),
        out_shape=jax.ShapeDtypeStruct((n, MLA_HEADS * V_HEAD_DIM), MM_DTYPE),
        grid=(n_seq, hpairs, qb),
        in_specs=in_specs,
        out_specs=pl.BlockSpec((tq, 2 * V_HEAD_DIM), lambda b, h, i: (b * qb + i, h)),
        compiler_params=_params("parallel", "parallel", "arbitrary"),
        name="mla_attn",
    )(*args)


def _post_store(z, mod_ref, lng_ref, lnb_ref, x1_ref, h2_ref):
    x1 = _layer_norm(z, lng_ref[...], lnb_ref[...])
    x1_ref[...] = x1
    h2_ref[...] = (x1 * (1.0 + mod_ref[0, 4:5, :]) + mod_ref[0, 3:4, :]).astype(h2_ref.dtype)


def _mla_post_kernel(o_ref, x_ref, mod_ref, wo_ref, lng_ref, lnb_ref, x1_ref, h2_ref):
    y = _nn(o_ref[...], wo_ref[...])
    z = DEEPNORM_ALPHA * x_ref[...] + mod_ref[0, 2:3, :] * y
    _post_store(z, mod_ref, lng_ref, lnb_ref, x1_ref, h2_ref)


def _mla_post(o, x, mod, cond_fn, wo, lng, lnb, tile):
    n = x.shape[0]
    row = lambda i: (i, 0)
    return pl.pallas_call(
        _mla_post_kernel,
        out_shape=(jax.ShapeDtypeStruct((n, D_MODEL), F32), jax.ShapeDtypeStruct((n, D_MODEL), MM_DTYPE)),
        grid=(n // tile,),
        in_specs=[
            pl.BlockSpec((tile, D_MODEL), row), pl.BlockSpec((tile, D_MODEL), row), _mod_spec(cond_fn),
            _full(wo.shape), _full(lng.shape), _full(lnb.shape),
        ],
        out_specs=(pl.BlockSpec((tile, D_MODEL), row), pl.BlockSpec((tile, D_MODEL), row)),
        compiler_params=_params("parallel"),
        name="mla_post",
    )(o, x, mod, wo, lng, lnb)


def _pool_kernel(x_ref, xp_ref, xn_ref, mod_ref, w_ref, ps_ref, lng_ref, lnb_ref, x1_ref, h2_ref, ext_ref,
                 *, tile, tiles_per_seq):
    i = pl.program_id(0)
    pos = i % tiles_per_seq
    scale1 = 1.0 + mod_ref[0, 1:2, :]
    shift1 = mod_ref[0, 0:1, :]
    x = x_ref[...]
    keep_prev = (pos > 0).astype(F32)
    keep_next = (pos < tiles_per_seq - 1).astype(F32)
    ext_ref[0:POOL_HALO, :] = (xp_ref[...] * scale1 + shift1) * keep_prev
    ext_ref[POOL_HALO:POOL_HALO + tile, :] = x * scale1 + shift1
    ext_ref[POOL_HALO + tile:, :] = (xn_ref[...] * scale1 + shift1) * keep_next
    seq_len = tile * tiles_per_seq
    t = pos * tile + lax.broadcasted_iota(jnp.int32, (tile, POOL_GROUP_DIM), 0)
    ys = []
    for g, w in enumerate(POOL_WINDOWS):
        cs = slice(g * POOL_GROUP_DIM, (g + 1) * POOL_GROUP_DIM)
        tot = None
        for d in range(-(w // 2), w - w // 2):
            piece = ext_ref[POOL_HALO + d:POOL_HALO + d + tile, cs]
            tot = piece if tot is None else tot + piece
        lo = jnp.maximum(t - w // 2, 0)
        hi = jnp.minimum(t + (w - w // 2) - 1, seq_len - 1)
        cnt = (hi - lo + 1).astype(F32)
        diff = tot / cnt - ext_ref[POOL_HALO:POOL_HALO + tile, cs]
        ys.append(_nn(diff.astype(MM_DTYPE), w_ref[g]))
    y = jnp.concatenate(ys, axis=-1) * ps_ref[...]
    z = DEEPNORM_ALPHA * x + mod_ref[0, 2:3, :] * y
    _post_store(z, mod_ref, lng_ref, lnb_ref, x1_ref, h2_ref)


def _pool_layer(x, mod, cond_fn, seq_len, w, ps, lng, lnb, tile):
    n = x.shape[0]
    tps = seq_len // tile
    hb = tile // POOL_HALO
    nblk = n // POOL_HALO
    row = lambda i: (i, 0)
    return pl.pallas_call(
        functools.partial(_pool_kernel, tile=tile, tiles_per_seq=tps),
        out_shape=(jax.ShapeDtypeStruct((n, D_MODEL), F32), jax.ShapeDtypeStruct((n, D_MODEL), MM_DTYPE)),
        grid=(n // tile,),
        in_specs=[
            pl.BlockSpec((tile, D_MODEL), row),
            pl.BlockSpec((POOL_HALO, D_MODEL), lambda i: (jnp.maximum(i * hb - 1, 0), 0)),
            pl.BlockSpec((POOL_HALO, D_MODEL), lambda i: (jnp.minimum((i + 1) * hb, nblk - 1), 0)),
            _mod_spec(cond_fn), _full(w.shape), _full(ps.shape), _full(lng.shape), _full(lnb.shape),
        ],
        out_specs=(pl.BlockSpec((tile, D_MODEL), row), pl.BlockSpec((tile, D_MODEL), row)),
        scratch_shapes=[pltpu.VMEM((tile + 2 * POOL_HALO, D_MODEL), F32)],
        compiler_params=_params("parallel"),
        name="pool_layer",
    )(x, x, x, mod, w, ps, lng, lnb)


def _fnet_chan_kernel(x_ref, mod_ref, cc_ref, sc_ref, hc_ref, hs_ref):
    h = x_ref[...] * (1.0 + mod_ref[0, 1:2, :]) + mod_ref[0, 0:1, :]
    hb = h.astype(MM_DTYPE)
    for g in range(N_FOURIER_GROUPS):
        cs = slice(g * FOURIER_GROUP_DIM, (g + 1) * FOURIER_GROUP_DIM)
        hc_ref[:, cs] = _nn(hb[:, cs], cc_ref[...]).astype(hc_ref.dtype)
        hs_ref[:, cs] = _nn(hb[:, cs], sc_ref[...]).astype(hs_ref.dtype)


def _fnet_chan(x, mod, cond_fn, cc, sc, tile):
    n = x.shape[0]
    row = lambda i: (i, 0)
    return pl.pallas_call(
        _fnet_chan_kernel,
        out_shape=(jax.ShapeDtypeStruct((n, D_MODEL), MM_DTYPE), jax.ShapeDtypeStruct((n, D_MODEL), MM_DTYPE)),
        grid=(n // tile,),
        in_specs=[pl.BlockSpec((tile, D_MODEL), row), _mod_spec(cond_fn), _full(cc.shape), _full(sc.shape)],
        out_specs=(pl.BlockSpec((tile, D_MODEL), row), pl.BlockSpec((tile, D_MODEL), row)),
        compiler_params=_params("parallel"),
        name="fnet_chan",
    )(x, mod, cc, sc)


def _fnet_seq_kernel(cl_ref, sl_ref, hc_ref, hs_ref, x_ref, mod_ref, fw_ref, fb_ref, lng_ref, lnb_ref,
                     x1_ref, h2_ref, acc_ref):
    tt = pl.program_id(2)

    @pl.when(tt == 0)
    def _():
        acc_ref[...] = jnp.zeros_like(acc_ref)

    acc_ref[...] += _nn(cl_ref[...], hc_ref[...]) - _nn(sl_ref[...], hs_ref[...])

    @pl.when(tt == pl.num_programs(2) - 1)
    def _():
        y = _nn(acc_ref[...].astype(MM_DTYPE), fw_ref[...]) + fb_ref[...]
        z = DEEPNORM_ALPHA * x_ref[...] + mod_ref[0, 2:3, :] * y
        _post_store(z, mod_ref, lng_ref, lnb_ref, x1_ref, h2_ref)


def _fnet_seq(hc, hs, x, mod, per_seq, n_seq, seq_len, cl, sl, fw, fb, lng, lnb, tk, tt):
    n = x.shape[0]
    kb = seq_len // tk
    tb = seq_len // tt
    out_row = lambda b, k, t: (b * kb + k, 0)
    cond = (lambda b, k, t: (1 + b, 0, 0)) if per_seq else (lambda b, k, t: (0, 0, 0))
    return pl.pallas_call(
        _fnet_seq_kernel,
        out_shape=(jax.ShapeDtypeStruct((n, D_MODEL), F32), jax.ShapeDtypeStruct((n, D_MODEL), MM_DTYPE)),
        grid=(n_seq, kb, tb),
        in_specs=[
            pl.BlockSpec((tk, tt), lambda b, k, t: (k, t)),
            pl.BlockSpec((tk, tt), lambda b, k, t: (k, t)),
            pl.BlockSpec((tt, D_MODEL), lambda b, k, t: (b * tb + t, 0)),
            pl.BlockSpec((tt, D_MODEL), lambda b, k, t: (b * tb + t, 0)),
            pl.BlockSpec((tk, D_MODEL), out_row),
            pl.BlockSpec((1, 6, D_MODEL), cond),
            _full(fw.shape), _full(fb.shape), _full(lng.shape), _full(lnb.shape),
        ],
        out_specs=(pl.BlockSpec((tk, D_MODEL), out_row), pl.BlockSpec((tk, D_MODEL), out_row)),
        scratch_shapes=[pltpu.VMEM((tk, D_MODEL), F32)],
        compiler_params=_params("parallel", "parallel", "arbitrary"),
        name="fnet_seq",
    )(cl, sl, hc, hs, x, mod, fw, fb, lng, lnb)


def _top16(x):
    v = _sort_desc([x[SUBLANES * r:SUBLANES * (r + 1)] for r in range(x.shape[0] // SUBLANES)])
    rows = []
    for k in range(PEER_TOPK):
        m = jnp.max(v[0], axis=0, keepdims=True)
        rows.append(m)
        hit = v[0] == m
        v = [jnp.where(hit, v[d + 1], v[d]) for d in range(PEER_TOPK - 1 - k)]
    return rows


def _sort_desc(v):
    v = list(v)
    n = len(v)
    p = 1
    while p < n:
        k = p
        while k >= 1:
            for j in range(k % p, n - k, 2 * k):
                for i in range(min(k, n - j - k)):
                    if (i + j) // (2 * p) == (i + j + k) // (2 * p):
                        a, b = v[i + j], v[i + j + k]
                        v[i + j], v[i + j + k] = jnp.maximum(a, b), jnp.minimum(a, b)
            k //= 2
        p *= 2
    return v


def _stack8(rows, row8):
    out = jnp.broadcast_to(rows[0], row8.shape)
    for k in range(1, 8):
        out = jnp.where(row8 == k, rows[k], out)
    return out


def _peer_topk_kernel(h_ref, wqt_ref, sk_ref, e2_ref, th_ref, e1_ref, qt_ref, *, tile):
    qt_ref[...] = _nt(wqt_ref[...], h_ref[...]).astype(qt_ref.dtype)
    row8 = lax.broadcasted_iota(jnp.int32, (8, LANES), 0)

    def head(hd, carry):
        base = pl.multiple_of(hd * PEER_QUERY_DIM, PEER_QUERY_DIM)
        for lg in range(tile // LANES):
            ls = slice(lg * LANES, (lg + 1) * LANES)
            s1 = _nn(sk_ref[2 * hd], qt_ref[pl.ds(base, PEER_HALF), ls])
            s2 = _nn(sk_ref[2 * hd + 1], qt_ref[pl.ds(base + PEER_HALF, PEER_HALF), ls])
            t1 = _top16(s1)
            t2 = _top16(s2)
            t1a = _stack8(t1[:8], row8)
            lists = [t1a + t2[0]] + [jnp.where(row8 < PEER_TOPK // (l + 1), t1a + t2[l], -jnp.inf)
                                     for l in range(1, PEER_TOPK)]
            tail = _stack8(t1[8:], row8) + t2[0]
            top = t1[0] + t2[0]
            z = jnp.zeros_like(top)
            tau = top
            for j in range(PEER_TOPK):
                tau = jnp.max(jnp.maximum(lists[0], tail), axis=0, keepdims=True)
                z = z + jnp.exp(tau - top)
                if j < PEER_TOPK - 1:
                    hit = lists[0] == tau
                    tail = jnp.where(tail == tau, -jnp.inf, tail)
                    lists = [jnp.where(hit, lists[d + 1], lists[d]) for d in range(len(lists) - 1)]
            e2_of = lambda s: jnp.exp(s - t2[0]) * (0.5 / z)
            theta = jnp.full_like(s1, jnp.inf)
            for l in range(PEER_TOPK // 2):
                theta = jnp.where(s1 + t2[l] >= tau, e2_of(t2[l]), theta)
            best = jnp.full_like(top, jnp.inf)
            for l in range(PEER_TOPK // 2, PEER_TOPK):
                best = jnp.where(t1[0] + t2[l] >= tau, e2_of(t2[l]), best)
            theta = jnp.where(s1 == t1[0], jnp.minimum(theta, best), theta)
            e2_ref[hd, lg] = e2_of(s2)
            th_ref[hd, lg] = theta
            e1_ref[hd, lg] = jnp.exp(s1 - t1[0])
        return carry

    lax.fori_loop(0, PEER_HEADS, head, 0)


def _peer_topk(h2, wqt, sk, tile):
    n = h2.shape[0]
    g = tile // LANES
    shp = jax.ShapeDtypeStruct((PEER_HEADS, n // LANES, PEER_N_KEYS, LANES), F32)
    spec = pl.BlockSpec((PEER_HEADS, g, PEER_N_KEYS, LANES), lambda i: (0, i, 0, 0))
    return pl.pallas_call(
        functools.partial(_peer_topk_kernel, tile=tile),
        out_shape=(shp, shp, shp),
        grid=(n // tile,),
        in_specs=[pl.BlockSpec((tile, D_MODEL), lambda i: (i, 0)), _full(wqt.shape), _full(sk.shape)],
        out_specs=(spec, spec, spec),
        scratch_shapes=[pltpu.VMEM((PEER_HEADS * PEER_QUERY_DIM, tile), MM_DTYPE)],
        compiler_params=_params("parallel"),
        name="peer_topk",
    )(h2, wqt, sk)


def _peer_main_kernel(h_ref, u_ref, vt_ref, e2_ref, th_ref, e1_ref, x1_ref, mod_ref, lng_ref, lnb_ref,
                      out_ref, at_ref, hm_ref, acc_ref, *, tile, eblk):
    j = pl.program_id(1)

    @pl.when(j == 0)
    def _():
        acc_ref[...] = jnp.zeros_like(acc_ref)

    at_ref[...] = _nt(u_ref[...], h_ref[...])
    for lg in range(tile // LANES):
        ls = slice(lg * LANES, (lg + 1) * LANES)
        for il in range(eblk // PEER_N_KEYS):
            rs = slice(il * PEER_N_KEYS, (il + 1) * PEER_N_KEYS)
            gate = None
            for hd in range(PEER_HEADS):
                e2 = e2_ref[hd, lg]
                w = jnp.where(e2 >= th_ref[hd, lg, il:il + 1, :], e1_ref[hd, lg, il:il + 1, :] * e2, 0.0)
                gate = w if gate is None else gate + w
            a = at_ref[rs, ls]
            hm_ref[rs, ls] = (gate * a * (1.0 + lax.erf(a * (2.0 ** -0.5)))).astype(hm_ref.dtype)
    acc_ref[...] += _nn(vt_ref[...], hm_ref[...])

    @pl.when(j == pl.num_programs(1) - 1)
    def _():
        f = acc_ref[...].T
        z = DEEPNORM_ALPHA * x1_ref[...] + mod_ref[0, 5:6, :] * f
        out_ref[...] = _layer_norm(z, lng_ref[...], lnb_ref[...])


def _peer_main(h2, u, vt, e2, th, e1, x1, mod, cond_fn, lng, lnb, tile, eblk):
    n = h2.shape[0]
    g = tile // LANES
    ik = eblk // PEER_N_KEYS
    row = lambda i, j: (i, 0)
    full_keys = pl.BlockSpec((PEER_HEADS, g, PEER_N_KEYS, LANES), lambda i, j: (0, i, 0, 0))
    slab_keys = pl.BlockSpec((PEER_HEADS, g, ik, LANES), lambda i, j: (0, i, j, 0))
    return pl.pallas_call(
        functools.partial(_peer_main_kernel, tile=tile, eblk=eblk),
        out_shape=jax.ShapeDtypeStruct((n, D_MODEL), F32),
        grid=(n // tile, PEER_N_EXPERTS // eblk),
        in_specs=[
            pl.BlockSpec((tile, D_MODEL), row),
            pl.BlockSpec((eblk, D_MODEL), lambda i, j: (j, 0)),
            pl.BlockSpec((D_MODEL, eblk), lambda i, j: (0, j)),
            full_keys, slab_keys, slab_keys,
            pl.BlockSpec((tile, D_MODEL), row),
            pl.BlockSpec((1, 6, D_MODEL), lambda i, j: (cond_fn(i), 0, 0)),
            pl.BlockSpec(lng.shape, lambda i, j: (0, 0)), pl.BlockSpec(lnb.shape, lambda i, j: (0, 0)),
        ],
        out_specs=pl.BlockSpec((tile, D_MODEL), row),
        scratch_shapes=[pltpu.VMEM((eblk, tile), F32), pltpu.VMEM((eblk, tile), MM_DTYPE),
                        pltpu.VMEM((D_MODEL, tile), F32)],
        compiler_params=_params("parallel", "arbitrary"),
        name="peer_main",
    )(h2, u, vt, e2, th, e1, x1, mod, lng, lnb)


def _peer_layer(h2, x1, mod, cond_fn, pw, lng, lnb, tile_topk, tile_main, eblk):
    e2, th, e1 = _peer_topk(h2, pw["wqt"], pw["sk"], tile_topk)
    return _peer_main(h2, pw["u"], pw["vt"], e2, th, e1, x1, mod, cond_fn, lng, lnb, tile_main, eblk)


def _pad_heads(w, n_used):
    k = w.shape[0]
    return jnp.pad(w, ((0, 0), (0, 0), (0, HEAD_PAD - n_used))).reshape(k, MLA_HEADS * HEAD_PAD)


def _rotate_half_cols(w):
    n = QK_ROPE_DIM // 4
    w4 = w.reshape(w.shape[:-1] + (2, 2, n))
    return jnp.stack([-w4[..., 1, :], w4[..., 0, :]], axis=-2).reshape(w.shape)


def _mla_weights(wdq, gq, wuq, wdkv, gkv, wukv):
    dq = QK_NOPE_DIM + QK_ROPE_DIM
    wuq3 = wuq.reshape(Q_LORA_RANK, MLA_HEADS, dq)
    pe = wuq3[..., QK_NOPE_DIM:]
    zeros_nope = jnp.zeros((Q_LORA_RANK, MLA_HEADS, QK_NOPE_DIM), F32)
    wuq_pad = _pad_heads(wuq3, dq)
    wuqs_pad = _pad_heads(jnp.concatenate([zeros_nope, _rotate_half_cols(pe)], axis=-1), dq)
    wkpe = wdkv[:, KV_LORA_RANK:]
    zk = jnp.zeros((D_MODEL, ROPE_OFF), F32)
    zt = jnp.zeros((D_MODEL, HEAD_PAD - ROPE_OFF - QK_ROPE_DIM), F32)
    wdkv_pad = jnp.concatenate([wdkv[:, :KV_LORA_RANK], zk, wkpe, zt, zk, _rotate_half_cols(wkpe), zt], axis=-1)
    wukv3 = wukv.reshape(KV_LORA_RANK, MLA_HEADS, QK_NOPE_DIM + V_HEAD_DIM)
    place = jnp.pad(jnp.eye(QK_ROPE_DIM, dtype=F32), ((0, 0), (ROPE_OFF, HEAD_PAD - ROPE_OFF - QK_ROPE_DIM)))
    return dict(
        wdq=wdq.astype(MM_DTYPE), gq=gq.reshape(1, -1), wuq=wuq_pad.astype(MM_DTYPE), wuqs=wuqs_pad.astype(MM_DTYPE),
        wdkv=wdkv_pad.astype(MM_DTYPE), gkv=gkv.reshape(1, -1),
        wukvk=_pad_heads(wukv3[..., :QK_NOPE_DIM], QK_NOPE_DIM).astype(MM_DTYPE),
        wukvv=wukv3[..., QK_NOPE_DIM:].reshape(KV_LORA_RANK, MLA_HEADS * V_HEAD_DIM).astype(MM_DTYPE),
        place=place.astype(MM_DTYPE),
    )


def _rope_tables(length):
    rows = length // GRID_W
    t = jnp.arange(rows * GRID_W)
    row = (t // GRID_W).astype(F32)
    col = (t % GRID_W).astype(F32)
    half = QK_ROPE_DIM // 2
    inv_freq = 1.0 / jnp.power(ROPE_THETA, jnp.arange(0, half, 2, dtype=F32) / half)
    n = QK_ROPE_DIM // 4
    ang = jnp.stack([row[:, None] * inv_freq, col[:, None] * inv_freq], axis=1)
    ang = jnp.broadcast_to(ang[:, :, None, :], (length, 2, 2, n)).reshape(length, QK_ROPE_DIM)
    pad = ((0, 0), (ROPE_OFF, HEAD_PAD - ROPE_OFF - QK_ROPE_DIM))
    cos = jnp.pad(jnp.cos(ang), pad, constant_values=1.0)
    sin = jnp.pad(jnp.sin(ang), pad)
    return cos, sin


def _dft_tables(length, scale):
    k = jnp.arange(length, dtype=jnp.int32)
    kt = (k[:, None] * k[None, :]) % length
    ang = kt.astype(F32) * (2.0 * math.pi / length)
    return (jnp.cos(ang) * scale).astype(MM_DTYPE), (jnp.sin(ang) * scale).astype(MM_DTYPE)


def kernel(x_prompt, x_sample, cache_ckv, cache_kpe, c, c_ctx, ada_w, ada_b, ln_mix_g, ln_mix_b, ln_ffn_g, ln_ffn_b, mla_wdq, mla_q_norm_g, mla_wuq, mla_wdkv, mla_kv_norm_g, mla_wukv, mla_wo, pool_w, pool_scale, fnet_w, fnet_b, peer_wq, peer_subkeys, peer_u, peer_v):
    nb_p, len_p, _ = x_prompt.shape
    nb_s, len_s, _ = x_sample.shape
    assert 1 + nb_s <= N_COND_ROWS
    cond = jnp.zeros((N_COND_ROWS, D_MODEL), F32).at[0].set(c_ctx).at[1:1 + nb_s].set(c)
    mods = _adaln(cond, ada_w, ada_b)

    streams = [
        dict(x=x_prompt.reshape(nb_p * len_p, D_MODEL), n_seq=nb_p, seq_len=len_p, per_seq=False),
        dict(x=x_sample.reshape(nb_s * len_s, D_MODEL), n_seq=nb_s, seq_len=len_s, per_seq=True),
    ]
    cos_s, sin_s = _rope_tables(len_s)
    new_ckv, new_kpe = [], []

    for i in range(DEPTH):
        kind, j = i % N_MIXERS, i // N_MIXERS
        mod = mods[i]
        lng, lnb = ln_mix_g[i].reshape(1, -1), ln_mix_b[i].reshape(1, -1)
        pw = dict(
            wqt=peer_wq[i].T.astype(MM_DTYPE),
            sk=peer_subkeys[i].reshape(PEER_HEADS * 2, PEER_N_KEYS, PEER_HALF).astype(MM_DTYPE),
            u=peer_u[i].astype(MM_DTYPE),
            vt=peer_v[i].T.astype(MM_DTYPE),
        )
        if kind == 0:
            mw = _mla_weights(mla_wdq[j], mla_q_norm_g[j], mla_wuq[j], mla_wdkv[j], mla_kv_norm_g[j], mla_wukv[j])
            wo = mla_wo[j].astype(MM_DTYPE)
            ctx = _mla_ctx(cache_ckv, cache_kpe, j, mw)
        elif kind == 2:
            cc, sc = _dft_tables(FOURIER_GROUP_DIM, FOURIER_GROUP_DIM ** -0.5)
        for s in streams:
            x, seq_len, n_seq = s["x"], s["seq_len"], s["n_seq"]
            tile = min(TOKEN_TILE, seq_len)
            cond_fn = _cond_fn(s["per_seq"], tile, seq_len)
            if kind == 0:
                if s["per_seq"]:
                    tps = seq_len // tile
                    q, k, v, _, _ = _mla_proj(x, mod, cond_fn, cos_s, sin_s, lambda t, tps=tps: t % tps, mw, tile)
                    o = _attention(q, k, v, n_seq, seq_len, min(ATTN_Q_TILE, seq_len), ctx=ctx)
                else:
                    no_pos = (jnp.ones((tile, HEAD_PAD), F32), jnp.zeros((tile, HEAD_PAD), F32))
                    q, k, v, ckv, kpe = _mla_proj(x, mod, cond_fn, *no_pos, lambda t: 0, mw, tile)
                    o = _attention(q, k, v, n_seq, seq_len, min(ATTN_Q_TILE, seq_len))
                    new_ckv.append(ckv.reshape(n_seq, seq_len, KV_LORA_RANK))
                    new_kpe.append(kpe.reshape(n_seq, seq_len, QK_ROPE_DIM))
                x1, h2 = _mla_post(o, x, mod, cond_fn, wo, lng, lnb, tile)
            elif kind == 1:
                x1, h2 = _pool_layer(x, mod, cond_fn, seq_len, pool_w[j].astype(MM_DTYPE),
                                     pool_scale[j].reshape(1, -1), lng, lnb, tile)
            else:
                hc, hs = _fnet_chan(x, mod, cond_fn, cc, sc, tile)
                cl, sl = _dft_tables(seq_len, seq_len ** -0.5)
                x1, h2 = _fnet_seq(hc, hs, x, mod, s["per_seq"], n_seq, seq_len, cl, sl,
                                   fnet_w[j].astype(MM_DTYPE), fnet_b[j].reshape(1, -1), lng, lnb,
                                   min(FNET_OUT_TILE, seq_len), min(FNET_IN_TILE, seq_len))
            tile_main = min(PEER_TOKEN_TILE, seq_len if s["per_seq"] else x.shape[0])
            cond_main = _cond_fn(s["per_seq"], tile_main, seq_len)
            s["x"] = _peer_layer(h2, x1, mod, cond_main, pw, ln_ffn_g[i].reshape(1, -1), ln_ffn_b[i].reshape(1, -1),
                                 min(PEER_TOPK_TILE, x.shape[0]), tile_main, PEER_EXPERT_SLAB)

    y_prompt = streams[0]["x"].reshape(nb_p, len_p, D_MODEL)
    y_sample = streams[1]["x"].reshape(nb_s, len_s, D_MODEL)
    return (y_prompt, y_sample, jnp.stack(new_ckv, axis=1), jnp.stack(new_kpe, axis=1))
```

```python
import functools
import math

import jax
import jax.numpy as jnp
from jax import lax
from jax.experimental import pallas as pl
from jax.experimental.pallas import tpu as pltpu

F32 = jnp.float32
MM_DTYPE = jnp.bfloat16

D_MODEL = 1024
DEPTH = 4
GRID_W = 64
N_MIXERS = 3

MLA_HEADS = 16
QK_NOPE_DIM = 64
QK_ROPE_DIM = 32
V_HEAD_DIM = 64
Q_LORA_RANK = 384
KV_LORA_RANK = 256
ROPE_THETA = 10000.0
HEAD_PAD = 128
ROPE_OFF = QK_NOPE_DIM

POOL_WINDOWS = (2, 4, 8, 16)
N_POOL_GROUPS = 4
POOL_GROUP_DIM = D_MODEL // N_POOL_GROUPS
POOL_HALO = 8

N_FOURIER_GROUPS = 4
FOURIER_GROUP_DIM = D_MODEL // N_FOURIER_GROUPS

PEER_HEADS = 8
PEER_N_KEYS = 128
PEER_N_EXPERTS = PEER_N_KEYS * PEER_N_KEYS
PEER_QUERY_DIM = 256
PEER_HALF = PEER_QUERY_DIM // 2
PEER_TOPK = 16

DEEPNORM_ALPHA = (2 * DEPTH) ** 0.25
LN_EPS = 1e-5
RMS_EPS = 1e-6

LANES = 128
SUBLANES = 8
N_COND_ROWS = 16
VMEM_LIMIT = 56 * 1024 * 1024

TOKEN_TILE = 512
ATTN_Q_TILE = 512
ATTN_HEADS_PER_STEP = 2
FNET_OUT_TILE = 512
FNET_IN_TILE = 1024
PEER_TOPK_TILE = 1024
PEER_TOKEN_TILE = 1024
PEER_EXPERT_SLAB = 1024


def _nn(a, b):
    return jnp.dot(a, b, preferred_element_type=F32)


def _nt(a, b):
    return lax.dot_general(a, b, (((1,), (1,)), ((), ())), preferred_element_type=F32)


def _params(*sem, flags=None):
    return pltpu.CompilerParams(dimension_semantics=sem, vmem_limit_bytes=VMEM_LIMIT, flags=flags)


def _layer_norm(z, g, b):
    mu = jnp.mean(z, axis=-1, keepdims=True)
    d = z - mu
    var = jnp.mean(d * d, axis=-1, keepdims=True)
    return d * lax.rsqrt(var + LN_EPS) * g + b


def _rms_norm(x, g):
    return x * lax.rsqrt(jnp.mean(x * x, axis=-1, keepdims=True) + RMS_EPS) * g


def _full(shape):
    n = len(shape)
    return pl.BlockSpec(shape, lambda *_: (0,) * n)


def _mod_spec(cond_fn):
    return pl.BlockSpec((1, 6, D_MODEL), lambda i, *_: (cond_fn(i), 0, 0))


def _cond_fn(per_seq, tile, seq_len):
    if not per_seq:
        return lambda i: 0
    return lambda i: 1 + (i * tile) // seq_len


def _adaln_kernel(cond_ref, w_ref, b_ref, o_ref):
    c = cond_ref[...]
    c = c / (1.0 + jnp.exp(-c))
    o_ref[0] = jnp.dot(c, w_ref[0], preferred_element_type=F32,
                       precision=lax.Precision.HIGHEST) + b_ref[0]


def _adaln(cond, ada_w, ada_b):
    nb = 1536
    out = pl.pallas_call(
        _adaln_kernel,
        out_shape=jax.ShapeDtypeStruct((DEPTH, N_COND_ROWS, 6 * D_MODEL), F32),
        grid=(DEPTH, 6 * D_MODEL // nb),
        in_specs=[
            pl.BlockSpec((N_COND_ROWS, D_MODEL), lambda i, j: (0, 0)),
            pl.BlockSpec((1, D_MODEL, nb), lambda i, j: (i, 0, j)),
            pl.BlockSpec((1, 1, nb), lambda i, j: (i, 0, j)),
        ],
        out_specs=pl.BlockSpec((1, N_COND_ROWS, nb), lambda i, j: (i, 0, j)),
        compiler_params=_params("parallel", "parallel"),
        name="adaln",
    )(cond, ada_w, ada_b.reshape(DEPTH, 1, 6 * D_MODEL))
    return out.reshape(DEPTH, N_COND_ROWS, 6, D_MODEL)


def _mla_proj_kernel(x_ref, mod_ref, cos_ref, sin_ref, wdq_ref, gq_ref, wuq_ref, wuqs_ref,
                     wdkv_ref, gkv_ref, wukvk_ref, wukvv_ref,
                     q_ref, k_ref, v_ref, ckv_ref, kpe_ref):
    h = x_ref[...] * (1.0 + mod_ref[0, 1:2, :]) + mod_ref[0, 0:1, :]
    hb = h.astype(MM_DTYPE)
    cq = _rms_norm(_nn(hb, wdq_ref[...]), gq_ref[...]).astype(MM_DTYPE)
    q = _nn(cq, wuq_ref[...])
    qs = _nn(cq, wuqs_ref[...])
    kv = _nn(hb, wdkv_ref[...])
    ckv = _rms_norm(kv[:, :KV_LORA_RANK], gkv_ref[...])
    kpe = kv[:, KV_LORA_RANK:KV_LORA_RANK + HEAD_PAD]
    kpes = kv[:, KV_LORA_RANK + HEAD_PAD:]
    cos = cos_ref[...]
    sin = sin_ref[...]
    kpe_rot = kpe * cos + kpes * sin
    ckvb = ckv.astype(MM_DTYPE)
    kn = _nn(ckvb, wukvk_ref[...])
    scale = (QK_NOPE_DIM + QK_ROPE_DIM) ** -0.5 * math.log2(math.e)
    for hd in range(MLA_HEADS):
        sl = slice(hd * HEAD_PAD, (hd + 1) * HEAD_PAD)
        q_ref[:, sl] = ((q[:, sl] * cos + qs[:, sl] * sin) * scale).astype(q_ref.dtype)
        k_ref[:, sl] = (kn[:, sl] + kpe_rot).astype(k_ref.dtype)
    v_ref[0], v_ref[1] = _split_values(_nn(ckvb, wukvv_ref[...]), v_ref.dtype)
    ckv_ref[...] = ckv
    kpe_ref[...] = kpe[:, ROPE_OFF:ROPE_OFF + QK_ROPE_DIM]


def _mla_proj(x, mod, cond_fn, cos, sin, pos_fn, w, tile):
    n = x.shape[0]
    hp = MLA_HEADS * HEAD_PAD
    hv = MLA_HEADS * V_HEAD_DIM
    row = lambda i: (i, 0)
    return pl.pallas_call(
        _mla_proj_kernel,
        out_shape=(
            jax.ShapeDtypeStruct((n, hp), MM_DTYPE),
            jax.ShapeDtypeStruct((n, hp), MM_DTYPE),
            jax.ShapeDtypeStruct((2, n, hv), MM_DTYPE),
            jax.ShapeDtypeStruct((n, KV_LORA_RANK), F32),
            jax.ShapeDtypeStruct((n, QK_ROPE_DIM), F32),
        ),
        grid=(n // tile,),
        in_specs=[
            pl.BlockSpec((tile, D_MODEL), row),
            _mod_spec(cond_fn),
            pl.BlockSpec((tile, HEAD_PAD), lambda i: (pos_fn(i), 0)),
            pl.BlockSpec((tile, HEAD_PAD), lambda i: (pos_fn(i), 0)),
            _full(w["wdq"].shape), _full(w["gq"].shape), _full(w["wuq"].shape), _full(w["wuqs"].shape),
            _full(w["wdkv"].shape), _full(w["gkv"].shape), _full(w["wukvk"].shape), _full(w["wukvv"].shape),
        ],
        out_specs=(
            pl.BlockSpec((tile, hp), row),
            pl.BlockSpec((tile, hp), row),
            pl.BlockSpec((2, tile, hv), lambda i: (0, i, 0)),
            pl.BlockSpec((tile, KV_LORA_RANK), row),
            pl.BlockSpec((tile, QK_ROPE_DIM), row),
        ),
        compiler_params=_params("parallel"),
        name="mla_proj",
    )(x, mod, cos, sin, w["wdq"], w["gq"], w["wuq"], w["wuqs"], w["wdkv"], w["gkv"], w["wukvk"], w["wukvv"])


def _mla_ctx_kernel(ckv_ref, kpe_ref, wukvk_ref, wukvv_ref, place_ref, k_ref, v_ref):
    ckvb = ckv_ref[...].astype(MM_DTYPE)
    kn = _nn(ckvb, wukvk_ref[...])
    kpe = _nn(kpe_ref[...].astype(MM_DTYPE), place_ref[...])
    for hd in range(MLA_HEADS):
        sl = slice(hd * HEAD_PAD, (hd + 1) * HEAD_PAD)
        k_ref[:, sl] = (kn[:, sl] + kpe).astype(k_ref.dtype)
    v_ref[0], v_ref[1] = _split_values(_nn(ckvb, wukvv_ref[...]), v_ref.dtype)


def _mla_ctx(cache_ckv, cache_kpe, j, w):
    nb, _, past, _ = cache_ckv.shape
    hp = MLA_HEADS * HEAD_PAD
    hv = MLA_HEADS * V_HEAD_DIM
    return pl.pallas_call(
        _mla_ctx_kernel,
        out_shape=(jax.ShapeDtypeStruct((nb * past, hp), MM_DTYPE),
                   jax.ShapeDtypeStruct((2, nb * past, hv), MM_DTYPE)),
        grid=(nb,),
        in_specs=[
            pl.BlockSpec((None, None, past, KV_LORA_RANK), lambda b: (b, j, 0, 0)),
            pl.BlockSpec((None, None, past, QK_ROPE_DIM), lambda b: (b, j, 0, 0)),
            _full(w["wukvk"].shape), _full(w["wukvv"].shape), _full(w["place"].shape),
        ],
        out_specs=(pl.BlockSpec((past, hp), lambda b: (b, 0)),
                   pl.BlockSpec((2, past, hv), lambda b: (0, b, 0))),
        compiler_params=_params("parallel"),
        name="mla_ctx",
    )(cache_ckv, cache_kpe, w["wukvk"], w["wukvv"], w["place"])


def _attn_kernel(*refs, has_ctx):
    if has_ctx:
        q_ref, k_ref, v_ref, kc_ref, vc_ref, o_ref = refs
    else:
        q_ref, k_ref, v_ref, o_ref = refs
    n_heads = q_ref.shape[1] // HEAD_PAD
    scores = []
    for hd in range(n_heads):
        sl = slice(hd * HEAD_PAD, (hd + 1) * HEAD_PAD)
        q = q_ref[:, sl]
        s = _nt(q, k_ref[:, sl])
        m = jnp.max(s, axis=-1, keepdims=True)
        sc = None
        if has_ctx:
            sc = _nt(q, kc_ref[:, sl])
            m = jnp.maximum(m, jnp.max(sc, axis=-1, keepdims=True))
        scores.append((s, sc, m))
    probs = []
    for s, sc, m in scores:
        p = jnp.exp2(s - m).astype(MM_DTYPE)
        pc = jnp.exp2(sc - m).astype(MM_DTYPE) if has_ctx else None
        probs.append((p, pc))
    outs = []
    for hd, (p, pc) in enumerate(probs):
        hh = hd % 2
        ps = slice((hd // 2) * 2 * V_HEAD_DIM, (hd // 2 + 1) * 2 * V_HEAD_DIM)
        o = _nn(p, v_ref[hh, :, ps])
        if has_ctx:
            o = o + _nn(pc, vc_ref[hh, :, ps])
        ol = _ones_lane(hh)
        outs.append(o * (1.0 / o[:, ol:ol + 1]))
    lane = lax.broadcasted_iota(jnp.int32, (1, 2 * V_HEAD_DIM), 1)
    for pr in range(n_heads // 2):
        ps = slice(pr * 2 * V_HEAD_DIM, (pr + 1) * 2 * V_HEAD_DIM)
        o_ref[:, ps] = jnp.where(lane < V_HEAD_DIM, outs[2 * pr], outs[2 * pr + 1]).astype(o_ref.dtype)


def _ones_lane(hh):
    return V_HEAD_DIM if hh == 0 else 0


def _split_values(v, dtype):
    lane = lax.broadcasted_iota(jnp.int32, (1, 2 * V_HEAD_DIM), 1)
    parts = []
    for hh in range(2):
        own = (lane < V_HEAD_DIM) if hh == 0 else (lane >= V_HEAD_DIM)
        fill = jnp.where(lane == _ones_lane(hh), 1.0, 0.0)
        blocks = [jnp.where(own, v[:, p * 2 * V_HEAD_DIM:(p + 1) * 2 * V_HEAD_DIM], fill)
                  for p in range(MLA_HEADS // 2)]
        parts.append(jnp.concatenate(blocks, axis=-1).astype(dtype))
    return parts


def _attention(q, k, v, n_seq, seq_len, tq, ctx=None):
    n = q.shape[0]
    nh = ATTN_HEADS_PER_STEP if seq_len > tq else MLA_HEADS
    qb = seq_len // tq
    in_specs = [
        pl.BlockSpec((tq, nh * HEAD_PAD), lambda b, h, i: (b * qb + i, h)),
        pl.BlockSpec((seq_len, nh * HEAD_PAD), lambda b, h, i: (b, h)),
        pl.BlockSpec((2, seq_len, nh * V_HEAD_DIM), lambda b, h, i: (0, b, h)),
    ]
    args = [q, k, v]
    if ctx is not None:
        kc, vc = ctx
        past = kc.shape[0] // n_seq
        in_specs += [
            pl.BlockSpec((past, nh * HEAD_PAD), lambda b, h, i: (b, h)),
            pl.BlockSpec((2, past, nh * V_HEAD_DIM), lambda b, h, i: (0, b, h)),
        ]
        args += [kc, vc]
    return pl.pallas_call(
        functools.partial(_attn_kernel, has_ctx=ctx is not None),
        out_shape=jax.ShapeDtypeStruct((n, MLA_HEADS * V_HEAD_DIM), MM_DTYPE),
        grid=(n_seq, MLA_HEADS // nh, qb),
        in_specs=in_specs,
        out_specs=pl.BlockSpec((tq, nh * V_HEAD_DIM), lambda b, h, i: (b * qb + i, h)),
        compiler_params=_params("parallel", "parallel", "arbitrary"),
        name="mla_attn",
    )(*args)


def _post_store(z, mod_ref, lng_ref, lnb_ref, x1_ref, h2_ref):
    x1 = _layer_norm(z, lng_ref[...], lnb_ref[...])
    x1_ref[...] = x1
    h2_ref[...] = (x1 * (1.0 + mod_ref[0, 4:5, :]) + mod_ref[0, 3:4, :]).astype(h2_ref.dtype)


def _mla_post_kernel(o_ref, x_ref, mod_ref, wo_ref, lng_ref, lnb_ref, x1_ref, h2_ref):
    y = _nn(o_ref[...], wo_ref[...])
    z = DEEPNORM_ALPHA * x_ref[...] + mod_ref[0, 2:3, :] * y
    _post_store(z, mod_ref, lng_ref, lnb_ref, x1_ref, h2_ref)


def _mla_post(o, x, mod, cond_fn, wo, lng, lnb, tile):
    n = x.shape[0]
    row = lambda i: (i, 0)
    return pl.pallas_call(
        _mla_post_kernel,
        out_shape=(jax.ShapeDtypeStruct((n, D_MODEL), F32), jax.ShapeDtypeStruct((n, D_MODEL), MM_DTYPE)),
        grid=(n // tile,),
        in_specs=[
            pl.BlockSpec((tile, D_MODEL), row), pl.BlockSpec((tile, D_MODEL), row), _mod_spec(cond_fn),
            _full(wo.shape), _full(lng.shape), _full(lnb.shape),
        ],
        out_specs=(pl.BlockSpec((tile, D_MODEL), row), pl.BlockSpec((tile, D_MODEL), row)),
        compiler_params=_params("parallel"),
        name="mla_post",
    )(o, x, mod, wo, lng, lnb)


def _pool_kernel(x_ref, xp_ref, xn_ref, mod_ref, w_ref, ps_ref, lng_ref, lnb_ref, x1_ref, h2_ref, ext_ref,
                 *, tile, tiles_per_seq):
    i = pl.program_id(0)
    pos = i % tiles_per_seq
    scale1 = 1.0 + mod_ref[0, 1:2, :]
    shift1 = mod_ref[0, 0:1, :]
    x = x_ref[...]
    keep_prev = (pos > 0).astype(F32)
    keep_next = (pos < tiles_per_seq - 1).astype(F32)
    ext_ref[0:POOL_HALO, :] = (xp_ref[...] * scale1 + shift1) * keep_prev
    ext_ref[POOL_HALO:POOL_HALO + tile, :] = x * scale1 + shift1
    ext_ref[POOL_HALO + tile:, :] = (xn_ref[...] * scale1 + shift1) * keep_next
    seq_len = tile * tiles_per_seq
    t = pos * tile + lax.broadcasted_iota(jnp.int32, (tile, POOL_GROUP_DIM), 0)
    ys = []
    for g, w in enumerate(POOL_WINDOWS):
        cs = slice(g * POOL_GROUP_DIM, (g + 1) * POOL_GROUP_DIM)
        tot = None
        for d in range(-(w // 2), w - w // 2):
            piece = ext_ref[POOL_HALO + d:POOL_HALO + d + tile, cs]
            tot = piece if tot is None else tot + piece
        lo = jnp.maximum(t - w // 2, 0)
        hi = jnp.minimum(t + (w - w // 2) - 1, seq_len - 1)
        cnt = (hi - lo + 1).astype(F32)
        diff = tot / cnt - ext_ref[POOL_HALO:POOL_HALO + tile, cs]
        ys.append(_nn(diff.astype(MM_DTYPE), w_ref[g]))
    y = jnp.concatenate(ys, axis=-1) * ps_ref[...]
    z = DEEPNORM_ALPHA * x + mod_ref[0, 2:3, :] * y
    _post_store(z, mod_ref, lng_ref, lnb_ref, x1_ref, h2_ref)


def _pool_layer(x, mod, cond_fn, seq_len, w, ps, lng, lnb, tile):
    n = x.shape[0]
    tps = seq_len // tile
    hb = tile // POOL_HALO
    nblk = n // POOL_HALO
    row = lambda i: (i, 0)
    return pl.pallas_call(
        functools.partial(_pool_kernel, tile=tile, tiles_per_seq=tps),
        out_shape=(jax.ShapeDtypeStruct((n, D_MODEL), F32), jax.ShapeDtypeStruct((n, D_MODEL), MM_DTYPE)),
        grid=(n // tile,),
        in_specs=[
            pl.BlockSpec((tile, D_MODEL), row),
            pl.BlockSpec((POOL_HALO, D_MODEL), lambda i: (jnp.maximum(i * hb - 1, 0), 0)),
            pl.BlockSpec((POOL_HALO, D_MODEL), lambda i: (jnp.minimum((i + 1) * hb, nblk - 1), 0)),
            _mod_spec(cond_fn), _full(w.shape), _full(ps.shape), _full(lng.shape), _full(lnb.shape),
        ],
        out_specs=(pl.BlockSpec((tile, D_MODEL), row), pl.BlockSpec((tile, D_MODEL), row)),
        scratch_shapes=[pltpu.VMEM((tile + 2 * POOL_HALO, D_MODEL), F32)],
        compiler_params=_params("parallel"),
        name="pool_layer",
    )(x, x, x, mod, w, ps, lng, lnb)


def _fnet_chan_kernel(x_ref, mod_ref, cc_ref, sc_ref, hc_ref, hs_ref):
    h = x_ref[...] * (1.0 + mod_ref[0, 1:2, :]) + mod_ref[0, 0:1, :]
    hb = h.astype(MM_DTYPE)
    for g in range(N_FOURIER_GROUPS):
        cs = slice(g * FOURIER_GROUP_DIM, (g + 1) * FOURIER_GROUP_DIM)
        hc_ref[:, cs] = _nn(hb[:, cs], cc_ref[...]).astype(hc_ref.dtype)
        hs_ref[:, cs] = _nn(hb[:, cs], sc_ref[...]).astype(hs_ref.dtype)


def _fnet_chan(x, mod, cond_fn, cc, sc, tile):
    n = x.shape[0]
    row = lambda i: (i, 0)
    return pl.pallas_call(
        _fnet_chan_kernel,
        out_shape=(jax.ShapeDtypeStruct((n, D_MODEL), MM_DTYPE), jax.ShapeDtypeStruct((n, D_MODEL), MM_DTYPE)),
        grid=(n // tile,),
        in_specs=[pl.BlockSpec((tile, D_MODEL), row), _mod_spec(cond_fn), _full(cc.shape), _full(sc.shape)],
        out_specs=(pl.BlockSpec((tile, D_MODEL), row), pl.BlockSpec((tile, D_MODEL), row)),
        compiler_params=_params("parallel"),
        name="fnet_chan",
    )(x, mod, cc, sc)


def _fnet_seq_kernel(cl_ref, sl_ref, hc_ref, hs_ref, x_ref, mod_ref, fw_ref, fb_ref, lng_ref, lnb_ref,
                     x1_ref, h2_ref, acc_ref):
    tt = pl.program_id(2)

    @pl.when(tt == 0)
    def _():
        acc_ref[...] = jnp.zeros_like(acc_ref)

    acc_ref[...] += _nn(cl_ref[...], hc_ref[...]) - _nn(sl_ref[...], hs_ref[...])

    @pl.when(tt == pl.num_programs(2) - 1)
    def _():
        y = _nn(acc_ref[...].astype(MM_DTYPE), fw_ref[...]) + fb_ref[...]
        z = DEEPNORM_ALPHA * x_ref[...] + mod_ref[0, 2:3, :] * y
        _post_store(z, mod_ref, lng_ref, lnb_ref, x1_ref, h2_ref)


def _fnet_seq(hc, hs, x, mod, per_seq, n_seq, seq_len, cl, sl, fw, fb, lng, lnb, tk, tt):
    n = x.shape[0]
    kb = seq_len // tk
    tb = seq_len // tt
    out_row = lambda b, k, t: (b * kb + k, 0)
    cond = (lambda b, k, t: (1 + b, 0, 0)) if per_seq else (lambda b, k, t: (0, 0, 0))
    return pl.pallas_call(
        _fnet_seq_kernel,
        out_shape=(jax.ShapeDtypeStruct((n, D_MODEL), F32), jax.ShapeDtypeStruct((n, D_MODEL), MM_DTYPE)),
        grid=(n_seq, kb, tb),
        in_specs=[
            pl.BlockSpec((tk, tt), lambda b, k, t: (k, t)),
            pl.BlockSpec((tk, tt), lambda b, k, t: (k, t)),
            pl.BlockSpec((tt, D_MODEL), lambda b, k, t: (b * tb + t, 0)),
            pl.BlockSpec((tt, D_MODEL), lambda b, k, t: (b * tb + t, 0)),
            pl.BlockSpec((tk, D_MODEL), out_row),
            pl.BlockSpec((1, 6, D_MODEL), cond),
            _full(fw.shape), _full(fb.shape), _full(lng.shape), _full(lnb.shape),
        ],
        out_specs=(pl.BlockSpec((tk, D_MODEL), out_row), pl.BlockSpec((tk, D_MODEL), out_row)),
        scratch_shapes=[pltpu.VMEM((tk, D_MODEL), F32)],
        compiler_params=_params("parallel", "parallel", "arbitrary"),
        name="fnet_seq",
    )(cl, sl, hc, hs, x, mod, fw, fb, lng, lnb)


def _top16(x):
    v = _sort_desc([x[SUBLANES * r:SUBLANES * (r + 1)] for r in range(x.shape[0] // SUBLANES)])
    rows = []
    for k in range(PEER_TOPK):
        m = jnp.max(v[0], axis=0, keepdims=True)
        rows.append(m)
        hit = v[0] == m
        v = [jnp.where(hit, v[d + 1], v[d]) for d in range(PEER_TOPK - 1 - k)]
    return rows


def _sort_desc(v):
    v = list(v)
    n = len(v)
    p = 1
    while p < n:
        k = p
        while k >= 1:
            for j in range(k % p, n - k, 2 * k):
                for i in range(min(k, n - j - k)):
                    if (i + j) // (2 * p) == (i + j + k) // (2 * p):
                        a, b = v[i + j], v[i + j + k]
                        v[i + j], v[i + j + k] = jnp.maximum(a, b), jnp.minimum(a, b)
            k //= 2
        p *= 2
    return v


def _stack8(rows, row8):
    out = jnp.broadcast_to(rows[0], row8.shape)
    for k in range(1, 8):
        out = jnp.where(row8 == k, rows[k], out)
    return out


def _peer_topk_kernel(h_ref, wqt_ref, sk_ref, e2_ref, th_ref, e1_ref, qt_ref, *, tile):
    qt_ref[...] = _nt(wqt_ref[...], h_ref[...]).astype(qt_ref.dtype)
    row8 = lax.broadcasted_iota(jnp.int32, (8, LANES), 0)

    def head(hd, carry):
        base = pl.multiple_of(hd * PEER_QUERY_DIM, PEER_QUERY_DIM)
        for lg in range(tile // LANES):
            ls = slice(lg * LANES, (lg + 1) * LANES)
            s1 = _nn(sk_ref[2 * hd], qt_ref[pl.ds(base, PEER_HALF), ls])
            s2 = _nn(sk_ref[2 * hd + 1], qt_ref[pl.ds(base + PEER_HALF, PEER_HALF), ls])
            t1 = _top16(s1)
            t2 = _top16(s2)
            t1a = _stack8(t1[:8], row8)
            lists = [t1a + t2[0]] + [jnp.where(row8 < PEER_TOPK // (l + 1), t1a + t2[l], -jnp.inf)
                                     for l in range(1, PEER_TOPK)]
            tail = _stack8(t1[8:], row8) + t2[0]
            top = t1[0] + t2[0]
            z = jnp.zeros_like(top)
            tau = top
            for j in range(PEER_TOPK):
                tau = jnp.max(jnp.maximum(lists[0], tail), axis=0, keepdims=True)
                z = z + jnp.exp(tau - top)
                if j < PEER_TOPK - 1:
                    hit = lists[0] == tau
                    tail = jnp.where(tail == tau, -jnp.inf, tail)
                    lists = [jnp.where(hit, lists[d + 1], lists[d]) for d in range(len(lists) - 1)]
            e2_of = lambda s: jnp.exp(s - t2[0]) * (0.5 / z)
            theta = jnp.full_like(s1, jnp.inf)
            for l in range(PEER_TOPK // 2):
                theta = jnp.where(s1 + t2[l] >= tau, e2_of(t2[l]), theta)
            best = jnp.full_like(top, jnp.inf)
            for l in range(PEER_TOPK // 2, PEER_TOPK):
                best = jnp.where(t1[0] + t2[l] >= tau, e2_of(t2[l]), best)
            theta = jnp.where(s1 == t1[0], jnp.minimum(theta, best), theta)
            e2_ref[hd, lg] = e2_of(s2)
            th_ref[hd, lg] = theta
            e1_ref[hd, lg] = jnp.exp(s1 - t1[0])
        return carry

    lax.fori_loop(0, PEER_HEADS, head, 0)


def _peer_topk(h2, wqt, sk, tile):
    n = h2.shape[0]
    g = tile // LANES
    shp = jax.ShapeDtypeStruct((PEER_HEADS, n // LANES, PEER_N_KEYS, LANES), F32)
    spec = pl.BlockSpec((PEER_HEADS, g, PEER_N_KEYS, LANES), lambda i: (0, i, 0, 0))
    return pl.pallas_call(
        functools.partial(_peer_topk_kernel, tile=tile),
        out_shape=(shp, shp, shp),
        grid=(n // tile,),
        in_specs=[pl.BlockSpec((tile, D_MODEL), lambda i: (i, 0)), _full(wqt.shape), _full(sk.shape)],
        out_specs=(spec, spec, spec),
        scratch_shapes=[pltpu.VMEM((PEER_HEADS * PEER_QUERY_DIM, tile), MM_DTYPE)],
        compiler_params=_params("parallel"),
        name="peer_topk",
    )(h2, wqt, sk)


def _peer_main_kernel(h_ref, u_ref, vt_ref, e2_ref, th_ref, e1_ref, x1_ref, mod_ref, lng_ref, lnb_ref,
                      out_ref, at_ref, hm_ref, acc_ref, *, tile, eblk):
    j = pl.program_id(1)

    @pl.when(j == 0)
    def _():
        acc_ref[...] = jnp.zeros_like(acc_ref)

    at_ref[...] = _nt(u_ref[...], h_ref[...])
    for lg in range(tile // LANES):
        ls = slice(lg * LANES, (lg + 1) * LANES)
        for il in range(eblk // PEER_N_KEYS):
            rs = slice(il * PEER_N_KEYS, (il + 1) * PEER_N_KEYS)
            gate = None
            for hd in range(PEER_HEADS):
                e2 = e2_ref[hd, lg]
                w = jnp.where(e2 >= th_ref[hd, lg, il:il + 1, :], e1_ref[hd, lg, il:il + 1, :] * e2, 0.0)
                gate = w if gate is None else gate + w
            a = at_ref[rs, ls]
            hm_ref[rs, ls] = (gate * a * (1.0 + lax.erf(a * (2.0 ** -0.5)))).astype(hm_ref.dtype)
    acc_ref[...] += _nn(vt_ref[...], hm_ref[...])

    @pl.when(j == pl.num_programs(1) - 1)
    def _():
        f = acc_ref[...].T
        z = DEEPNORM_ALPHA * x1_ref[...] + mod_ref[0, 5:6, :] * f
        out_ref[...] = _layer_norm(z, lng_ref[...], lnb_ref[...])


def _peer_main(h2, u, vt, e2, th, e1, x1, mod, cond_fn, lng, lnb, tile, eblk):
    n = h2.shape[0]
    g = tile // LANES
    ik = eblk // PEER_N_KEYS
    row = lambda i, j: (i, 0)
    full_keys = pl.BlockSpec((PEER_HEADS, g, PEER_N_KEYS, LANES), lambda i, j: (0, i, 0, 0))
    slab_keys = pl.BlockSpec((PEER_HEADS, g, ik, LANES), lambda i, j: (0, i, j, 0))
    return pl.pallas_call(
        functools.partial(_peer_main_kernel, tile=tile, eblk=eblk),
        out_shape=jax.ShapeDtypeStruct((n, D_MODEL), F32),
        grid=(n // tile, PEER_N_EXPERTS // eblk),
        in_specs=[
            pl.BlockSpec((tile, D_MODEL), row),
            pl.BlockSpec((eblk, D_MODEL), lambda i, j: (j, 0)),
            pl.BlockSpec((D_MODEL, eblk), lambda i, j: (0, j)),
            full_keys, slab_keys, slab_keys,
            pl.BlockSpec((tile, D_MODEL), row),
            pl.BlockSpec((1, 6, D_MODEL), lambda i, j: (cond_fn(i), 0, 0)),
            pl.BlockSpec(lng.shape, lambda i, j: (0, 0)), pl.BlockSpec(lnb.shape, lambda i, j: (0, 0)),
        ],
        out_specs=pl.BlockSpec((tile, D_MODEL), row),
        scratch_shapes=[pltpu.VMEM((eblk, tile), F32), pltpu.VMEM((eblk, tile), MM_DTYPE),
                        pltpu.VMEM((D_MODEL, tile), F32)],
        compiler_params=_params("parallel", "arbitrary"),
        name="peer_main",
    )(h2, u, vt, e2, th, e1, x1, mod, lng, lnb)


def _peer_layer(h2, x1, mod, cond_fn, pw, lng, lnb, tile_topk, tile_main, eblk):
    e2, th, e1 = _peer_topk(h2, pw["wqt"], pw["sk"], tile_topk)
    return _peer_main(h2, pw["u"], pw["vt"], e2, th, e1, x1, mod, cond_fn, lng, lnb, tile_main, eblk)


def _pad_heads(w, n_used):
    k = w.shape[0]
    return jnp.pad(w, ((0, 0), (0, 0), (0, HEAD_PAD - n_used))).reshape(k, MLA_HEADS * HEAD_PAD)


def _rotate_half_cols(w):
    n = QK_ROPE_DIM // 4
    w4 = w.reshape(w.shape[:-1] + (2, 2, n))
    return jnp.stack([-w4[..., 1, :], w4[..., 0, :]], axis=-2).reshape(w.shape)


def _mla_weights(wdq, gq, wuq, wdkv, gkv, wukv):
    dq = QK_NOPE_DIM + QK_ROPE_DIM
    wuq3 = wuq.reshape(Q_LORA_RANK, MLA_HEADS, dq)
    pe = wuq3[..., QK_NOPE_DIM:]
    zeros_nope = jnp.zeros((Q_LORA_RANK, MLA_HEADS, QK_NOPE_DIM), F32)
    wuq_pad = _pad_heads(wuq3, dq)
    wuqs_pad = _pad_heads(jnp.concatenate([zeros_nope, _rotate_half_cols(pe)], axis=-1), dq)
    wkpe = wdkv[:, KV_LORA_RANK:]
    zk = jnp.zeros((D_MODEL, ROPE_OFF), F32)
    zt = jnp.zeros((D_MODEL, HEAD_PAD - ROPE_OFF - QK_ROPE_DIM), F32)
    wdkv_pad = jnp.concatenate([wdkv[:, :KV_LORA_RANK], zk, wkpe, zt, zk, _rotate_half_cols(wkpe), zt], axis=-1)
    wukv3 = wukv.reshape(KV_LORA_RANK, MLA_HEADS, QK_NOPE_DIM + V_HEAD_DIM)
    place = jnp.pad(jnp.eye(QK_ROPE_DIM, dtype=F32), ((0, 0), (ROPE_OFF, HEAD_PAD - ROPE_OFF - QK_ROPE_DIM)))
    return dict(
        wdq=wdq.astype(MM_DTYPE), gq=gq.reshape(1, -1), wuq=wuq_pad.astype(MM_DTYPE), wuqs=wuqs_pad.astype(MM_DTYPE),
        wdkv=wdkv_pad.astype(MM_DTYPE), gkv=gkv.reshape(1, -1),
        wukvk=_pad_heads(wukv3[..., :QK_NOPE_DIM], QK_NOPE_DIM).astype(MM_DTYPE),
        wukvv=wukv3[..., QK_NOPE_DIM:].reshape(KV_LORA_RANK, MLA_HEADS * V_HEAD_DIM).astype(MM_DTYPE),
        place=place.astype(MM_DTYPE),
    )


def _rope_tables(length):
    rows = length // GRID_W
    t = jnp.arange(rows * GRID_W)
    row = (t // GRID_W).astype(F32)
    col = (t % GRID_W).astype(F32)
    half = QK_ROPE_DIM // 2
    inv_freq = 1.0 / jnp.power(ROPE_THETA, jnp.arange(0, half, 2, dtype=F32) / half)
    n = QK_ROPE_DIM // 4
    ang = jnp.stack([row[:, None] * inv_freq, col[:, None] * inv_freq], axis=1)
    ang = jnp.broadcast_to(ang[:, :, None, :], (length, 2, 2, n)).reshape(length, QK_ROPE_DIM)
    pad = ((0, 0), (ROPE_OFF, HEAD_PAD - ROPE_OFF - QK_ROPE_DIM))
    cos = jnp.pad(jnp.cos(ang), pad, constant_values=1.0)
    sin = jnp.pad(jnp.sin(ang), pad)
    return cos, sin


def _dft_tables(length, scale):
    k = jnp.arange(length, dtype=jnp.int32)
    kt = (k[:, None] * k[None, :]) % length
    ang = kt.astype(F32) * (2.0 * math.pi / length)
    return (jnp.cos(ang) * scale).astype(MM_DTYPE), (jnp.sin(ang) * scale).astype(MM_DTYPE)


def kernel(x_prompt, x_sample, cache_ckv, cache_kpe, c, c_ctx, ada_w, ada_b, ln_mix_g, ln_mix_b, ln_ffn_g, ln_ffn_b, mla_wdq, mla_q_norm_g, mla_wuq, mla_wdkv, mla_kv_norm_g, mla_wukv, mla_wo, pool_w, pool_scale, fnet_w, fnet_b, peer_wq, peer_subkeys, peer_u, peer_v):
    nb_p, len_p, _ = x_prompt.shape
    nb_s, len_s, _ = x_sample.shape
    assert 1 + nb_s <= N_COND_ROWS
    cond = jnp.zeros((N_COND_ROWS, D_MODEL), F32).at[0].set(c_ctx).at[1:1 + nb_s].set(c)
    mods = _adaln(cond, ada_w, ada_b)

    streams = [
        dict(x=x_prompt.reshape(nb_p * len_p, D_MODEL), n_seq=nb_p, seq_len=len_p, per_seq=False),
        dict(x=x_sample.reshape(nb_s * len_s, D_MODEL), n_seq=nb_s, seq_len=len_s, per_seq=True),
    ]
    cos_s, sin_s = _rope_tables(len_s)
    new_ckv, new_kpe = [], []

    for i in range(DEPTH):
        kind, j = i % N_MIXERS, i // N_MIXERS
        mod = mods[i]
        lng, lnb = ln_mix_g[i].reshape(1, -1), ln_mix_b[i].reshape(1, -1)
        pw = dict(
            wqt=peer_wq[i].T.astype(MM_DTYPE),
            sk=peer_subkeys[i].reshape(PEER_HEADS * 2, PEER_N_KEYS, PEER_HALF).astype(MM_DTYPE),
            u=peer_u[i].astype(MM_DTYPE),
            vt=peer_v[i].T.astype(MM_DTYPE),
        )
        if kind == 0:
            mw = _mla_weights(mla_wdq[j], mla_q_norm_g[j], mla_wuq[j], mla_wdkv[j], mla_kv_norm_g[j], mla_wukv[j])
            wo = mla_wo[j].astype(MM_DTYPE)
            ctx = _mla_ctx(cache_ckv, cache_kpe, j, mw)
        elif kind == 2:
            cc, sc = _dft_tables(FOURIER_GROUP_DIM, FOURIER_GROUP_DIM ** -0.5)
        for s in streams:
            x, seq_len, n_seq = s["x"], s["seq_len"], s["n_seq"]
            tile = min(TOKEN_TILE, seq_len)
            cond_fn = _cond_fn(s["per_seq"], tile, seq_len)
            if kind == 0:
                if s["per_seq"]:
                    tps = seq_len // tile
                    q, k, v, _, _ = _mla_proj(x, mod, cond_fn, cos_s, sin_s, lambda t, tps=tps: t % tps, mw, tile)
                    o = _attention(q, k, v, n_seq, seq_len, min(ATTN_Q_TILE, seq_len), ctx=ctx)
                else:
                    no_pos = (jnp.ones((tile, HEAD_PAD), F32), jnp.zeros((tile, HEAD_PAD), F32))
                    q, k, v, ckv, kpe = _mla_proj(x, mod, cond_fn, *no_pos, lambda t: 0, mw, tile)
                    o = _attention(q, k, v, n_seq, seq_len, min(ATTN_Q_TILE, seq_len))
                    new_ckv.append(ckv.reshape(n_seq, seq_len, KV_LORA_RANK))
                    new_kpe.append(kpe.reshape(n_seq, seq_len, QK_ROPE_DIM))
                x1, h2 = _mla_post(o, x, mod, cond_fn, wo, lng, lnb, tile)
            elif kind == 1:
                x1, h2 = _pool_layer(x, mod, cond_fn, seq_len, pool_w[j].astype(MM_DTYPE),
                                     pool_scale[j].reshape(1, -1), lng, lnb, tile)
            else:
                hc, hs = _fnet_chan(x, mod, cond_fn, cc, sc, tile)
                cl, sl = _dft_tables(seq_len, seq_len ** -0.5)
                x1, h2 = _fnet_seq(hc, hs, x, mod, s["per_seq"], n_seq, seq_len, cl, sl,
                                   fnet_w[j].astype(MM_DTYPE), fnet_b[j].reshape(1, -1), lng, lnb,
                                   min(FNET_OUT_TILE, seq_len), min(FNET_IN_TILE, seq_len))
            tile_main = min(PEER_TOKEN_TILE, seq_len if s["per_seq"] else x.shape[0])
            cond_main = _cond_fn(s["per_seq"], tile_main, seq_len)
            s["x"] = _peer_layer(h2, x1, mod, cond_main, pw, ln_ffn_g[i].reshape(1, -1), ln_ffn_b[i].reshape(1, -1),
                                 min(PEER_TOPK_TILE, x.shape[0]), tile_main, PEER_EXPERT_SLAB)

    y_prompt = streams[0]["x"].reshape(nb_p, len_p, D_MODEL)
    y_sample = streams[1]["x"].reshape(nb_s, len_s, D_MODEL)
    return (y_prompt, y_sample, jnp.stack(new_ckv, axis=1), jnp.stack(new_kpe, axis=1))
```
